```python
import jax, jax.numpy as jnp
from jax import lax
import numpy as np

D_MODEL = 1024
BATCH = 2
SEQ = 8192
DEPTH = 4
DEC_BATCH = 32
DEC_SEQ = 4
PAST_LEN = 8192
PAGE_SIZE = 128

N_HEADS = 8
HEAD_DIM = 64
D_ATTN = N_HEADS * HEAD_DIM
POOL_WINDOWS = (2, 4, 8, 16)
N_POOL_GROUPS = len(POOL_WINDOWS)
POOL_GROUP_DIM = 128
D_POOL = N_POOL_GROUPS * POOL_GROUP_DIM
POOL_BUF = max(POOL_WINDOWS) - 1
D_FF = 4 * D_MODEL
Q_BLOCK = 128
N_MOD = 6 * D_MODEL
RMS_EPS = 1e-6
FORGET_BIAS = 3.0
SPLITS = (D_ATTN, 2 * D_ATTN, 3 * D_ATTN, 3 * D_ATTN + N_HEADS,
          3 * D_ATTN + N_HEADS + D_POOL, 3 * D_ATTN + N_HEADS + D_POOL + D_MODEL)
N_IN = 3 * D_ATTN + N_HEADS + D_POOL + 2 * D_MODEL

kernel_name = "fox_pool_gated_hybrid_step"


def rms_norm(x, g):
    xf = x.astype(jnp.float32)
    y = xf * lax.rsqrt(jnp.mean(xf * xf, axis=-1, keepdims=True) + RMS_EPS)
    return (y * g.astype(jnp.float32)).astype(x.dtype)


def fox_attend(q, k, v, cq, ck, q_pos, k_pos):
    s = jnp.einsum('bthd,bshd->bhts', q, k, preferred_element_type=jnp.float32) * (HEAD_DIM ** -0.5)
    s = s + (jnp.transpose(cq, (0, 2, 1))[:, :, :, None] - jnp.transpose(ck, (0, 2, 1))[:, :, None, :])
    causal = k_pos[None, :] <= q_pos[:, None]
    s = jnp.where(causal[None, None], s, -jnp.inf)
    p = jax.nn.softmax(s, axis=-1)
    return jnp.einsum('bhts,bshd->bthd', p.astype(v.dtype), v)


def fox_attention_prompt(q, k, v, logf):
    B, T = q.shape[0], q.shape[1]
    nb = T // Q_BLOCK
    c = jnp.cumsum(logf.astype(jnp.float32), axis=1)
    k_pos = jnp.arange(T)
    qb = q.reshape(B, nb, Q_BLOCK, N_HEADS, HEAD_DIM).transpose(1, 0, 2, 3, 4)
    cb = c.reshape(B, nb, Q_BLOCK, N_HEADS).transpose(1, 0, 2, 3)
    starts = jnp.arange(nb) * Q_BLOCK

    def one_block(args):
        qi, ci, s0 = args
        return fox_attend(qi, k, v, ci, c, s0 + jnp.arange(Q_BLOCK), k_pos)

    out = lax.map(one_block, (qb, cb, starts))
    return out.transpose(1, 0, 2, 3, 4).reshape(B, T, N_HEADS, HEAD_DIM)


def fox_attention_sample(q, k, v, logf, k_past, v_past, logf_past):
    P, T = k_past.shape[1], q.shape[1]
    kk = jnp.concatenate([k_past.astype(k.dtype), k], axis=1)
    vv = jnp.concatenate([v_past.astype(v.dtype), v], axis=1)
    lf = jnp.concatenate([logf_past.astype(jnp.float32), logf.astype(jnp.float32)], axis=1)
    c = jnp.cumsum(lf, axis=1)
    return fox_attend(q, kk, vv, c[:, P:], c, P + jnp.arange(T), jnp.arange(P + T))


def multiscale_pool(u, prefix, pos0, w_mix, scale):
    B, T = u.shape[0], u.shape[1]
    ext = jnp.concatenate([prefix.astype(u.dtype), u], axis=1).astype(jnp.float32)
    cs = jnp.concatenate([jnp.zeros((B, 1, D_POOL), jnp.float32), jnp.cumsum(ext, axis=1)], axis=1)
    hi = cs[:, POOL_BUF + 1:]
    pos = pos0 + jnp.arange(T)
    means = []
    for g, w in enumerate(POOL_WINDOWS):
        sl = slice(g * POOL_GROUP_DIM, (g + 1) * POOL_GROUP_DIM)
        lo = cs[:, POOL_BUF + 1 - w:POOL_BUF + 1 - w + T, sl]
        cnt = jnp.minimum(pos + 1, w).astype(jnp.float32)
        means.append((hi[..., sl] - lo) / cnt[None, :, None])
    d = (jnp.concatenate(means, axis=-1) - u.astype(jnp.float32)).astype(u.dtype)
    d = d.reshape(B, T, N_POOL_GROUPS, POOL_GROUP_DIM)
    y = jnp.einsum('btgc,gce->btge', d, w_mix).reshape(B, T, D_POOL) * scale
    new_buf = ext[:, T:].astype(u.dtype)
    return y, new_buf


def trunk_layer(x, mod, attn_fn, pool_prefix, pool_pos0,
                g_pre_mix, w_in, b_forget, w_pool_mix, pool_scale, w_attn_up, w_pool_up,
                w_out, g_post_mix, g_pre_ffn, w_ff1, w_ff2, g_post_ffn):
    B, T = x.shape[0], x.shape[1]
    shift_m, scale_m, gate_m, shift_f, scale_f, gate_f = [mod[:, i][:, None, :] for i in range(6)]
    h = rms_norm(x, g_pre_mix) * (1 + scale_m) + shift_m
    z = h @ w_in
    q, k, v, f_pre, u, ga_pre, gb_pre = jnp.split(z, SPLITS, axis=-1)
    q = q.reshape(B, T, N_HEADS, HEAD_DIM)
    k = k.reshape(B, T, N_HEADS, HEAD_DIM)
    v = v.reshape(B, T, N_HEADS, HEAD_DIM)
    logf = jax.nn.log_sigmoid(f_pre.astype(jnp.float32) + b_forget.astype(jnp.float32))
    o_attn = attn_fn(q, k, v, logf).reshape(B, T, D_ATTN)
    o_pool, pool_buf = multiscale_pool(u, pool_prefix, pool_pos0, w_pool_mix, pool_scale)
    merged = jax.nn.sigmoid(ga_pre) * (o_attn @ w_attn_up) + jax.nn.sigmoid(gb_pre) * (o_pool @ w_pool_up)
    x = x + gate_m * rms_norm(merged @ w_out, g_post_mix)
    h2 = rms_norm(x, g_pre_ffn) * (1 + scale_f) + shift_f
    ff = jnp.square(jax.nn.relu(h2 @ w_ff1)) @ w_ff2
    x = x + gate_f * rms_norm(ff, g_post_ffn)
    return x, k, v, logf, pool_buf


def gather_pages(cache_l, page_table):
    g = cache_l[page_table]
    return g.reshape((g.shape[0], g.shape[1] * g.shape[2]) + g.shape[3:])


def setup_inputs(seed: int = 0) -> dict:
    key = jax.random.key(seed)
    ks = jax.random.split(key, 28)
    n_pages = PAST_LEN // PAGE_SIZE
    n_used = DEC_BATCH * n_pages
    n_phys = (5 * n_used) // 4
    nrm = jax.random.normal
    f32 = jnp.float32
    page_table = jax.random.permutation(ks[0], n_phys)[:n_used].reshape(DEC_BATCH, n_pages).astype(jnp.int32)
    return {
        "x_prompt": nrm(ks[1], (BATCH, SEQ, D_MODEL), f32),
        "x_sample": nrm(ks[2], (DEC_BATCH, DEC_SEQ, D_MODEL), f32),
        "cache_k": nrm(ks[3], (DEPTH, n_phys, PAGE_SIZE, N_HEADS, HEAD_DIM), f32),
        "cache_v": nrm(ks[4], (DEPTH, n_phys, PAGE_SIZE, N_HEADS, HEAD_DIM), f32),
        "cache_logf": jax.nn.log_sigmoid(FORGET_BIAS + nrm(ks[5], (DEPTH, n_phys, PAGE_SIZE, N_HEADS), f32)),
        "state_pool": nrm(ks[6], (DEPTH, DEC_BATCH, POOL_BUF, D_POOL), f32),
        "page_table": page_table,
        "c_prompt": nrm(ks[7], (BATCH, D_MODEL), f32),
        "c_sample": nrm(ks[8], (DEC_BATCH, D_MODEL), f32),
        "w_ada": nrm(ks[9], (DEPTH, D_MODEL, N_MOD), f32) * (0.5 * D_MODEL ** -0.5),
        "b_ada": nrm(ks[10], (DEPTH, N_MOD), f32) * 0.02,
        "g_pre_mix": 1.0 + 0.02 * nrm(ks[11], (DEPTH, D_MODEL), f32),
        "w_in": nrm(ks[12], (DEPTH, D_MODEL, N_IN), f32) * D_MODEL ** -0.5,
        "b_forget": FORGET_BIAS + 0.5 * nrm(ks[13], (DEPTH, N_HEADS), f32),
        "w_pool_mix": nrm(ks[14], (DEPTH, N_POOL_GROUPS, POOL_GROUP_DIM, POOL_GROUP_DIM), f32) * POOL_GROUP_DIM ** -0.5,
        "pool_scale": 1.0 + 0.1 * nrm(ks[15], (DEPTH, D_POOL), f32),
        "w_attn_up": nrm(ks[16], (DEPTH, D_ATTN, D_MODEL), f32) * D_ATTN ** -0.5,
        "w_pool_up": nrm(ks[17], (DEPTH, D_POOL, D_MODEL), f32) * D_POOL ** -0.5,
        "w_out": nrm(ks[18], (DEPTH, D_MODEL, D_MODEL), f32) * D_MODEL ** -0.5,
        "g_post_mix": 1.0 + 0.02 * nrm(ks[19], (DEPTH, D_MODEL), f32),
        "g_pre_ffn": 1.0 + 0.02 * nrm(ks[20], (DEPTH, D_MODEL), f32),
        "w_ff1": nrm(ks[21], (DEPTH, D_MODEL, D_FF), f32) * D_MODEL ** -0.5,
        "w_ff2": nrm(ks[22], (DEPTH, D_FF, D_MODEL), f32) * D_FF ** -0.5,
        "g_post_ffn": 1.0 + 0.02 * nrm(ks[23], (DEPTH, D_MODEL), f32),
    }


def reference(x_prompt, x_sample, cache_k, cache_v, cache_logf, state_pool, page_table,
              c_prompt, c_sample, w_ada, b_ada, g_pre_mix, w_in, b_forget, w_pool_mix,
              pool_scale, w_attn_up, w_pool_up, w_out, g_post_mix, g_pre_ffn, w_ff1, w_ff2,
              g_post_ffn):
    past_len = page_table.shape[1] * cache_k.shape[2]
    xp, xs = x_prompt, x_sample
    kp, vp, lp, pp, ksm, vsm, lsm, psm = [], [], [], [], [], [], [], []
    prompt_prefix = jnp.zeros((xp.shape[0], POOL_BUF, D_POOL), xp.dtype)
    for l in range(DEPTH):
        wl = (g_pre_mix[l], w_in[l], b_forget[l], w_pool_mix[l], pool_scale[l], w_attn_up[l],
              w_pool_up[l], w_out[l], g_post_mix[l], g_pre_ffn[l], w_ff1[l], w_ff2[l], g_post_ffn[l])
        mod_p = (jax.nn.silu(c_prompt) @ w_ada[l] + b_ada[l]).reshape(-1, 6, D_MODEL)
        mod_s = (jax.nn.silu(c_sample) @ w_ada[l] + b_ada[l]).reshape(-1, 6, D_MODEL)
        xp, k1, v1, lf1, pb1 = trunk_layer(xp, mod_p, fox_attention_prompt, prompt_prefix, 0, *wl)
        k_past = gather_pages(cache_k[l], page_table)
        v_past = gather_pages(cache_v[l], page_table)
        lf_past = gather_pages(cache_logf[l], page_table)
        attn_s = lambda q, k, v, lf, kp_=k_past, vp_=v_past, lp_=lf_past: fox_attention_sample(q, k, v, lf, kp_, vp_, lp_)
        xs, k2, v2, lf2, pb2 = trunk_layer(xs, mod_s, attn_s, state_pool[l], past_len, *wl)
        kp.append(k1); vp.append(v1); lp.append(lf1); pp.append(pb1)
        ksm.append(k2); vsm.append(v2); lsm.append(lf2); psm.append(pb2)
    y_prompt, y_sample = xp, xs
    k_prompt, v_prompt = jnp.stack(kp), jnp.stack(vp)
    logf_prompt, pool_prompt = jnp.stack(lp), jnp.stack(pp)
    k_sample, v_sample = jnp.stack(ksm), jnp.stack(vsm)
    logf_sample, pool_sample = jnp.stack(lsm), jnp.stack(psm)
    return (y_prompt, y_sample, k_prompt, v_prompt, logf_prompt, pool_prompt, k_sample, v_sample, logf_sample, pool_sample)
```

```python
import functools

import numpy as np
import jax
import jax.numpy as jnp
from jax import lax
from jax.experimental import pallas as pl
from jax.experimental.pallas import tpu as pltpu

F32 = jnp.float32
BF16 = jnp.bfloat16

N_HEADS = 8
HEAD_DIM = 64
D_ATTN = N_HEADS * HEAD_DIM
POOL_WINDOWS = (2, 4, 8, 16)
POOL_GROUP_DIM = 128
D_POOL = len(POOL_WINDOWS) * POOL_GROUP_DIM
POOL_BUF = max(POOL_WINDOWS) - 1
HALO = 16
RMS_EPS = 1e-6
NEG = -1e30
QK_SCALE = HEAD_DIM ** -0.5

VMEM_LIMIT = 48 * 1024 * 1024

_NT = (((1,), (1,)), ((), ()))


def _params(*sem):
    return pltpu.CompilerParams(dimension_semantics=sem, vmem_limit_bytes=VMEM_LIMIT)


def _rms(x, g):
    return x * lax.rsqrt(jnp.mean(x * x, axis=-1, keepdims=True) + RMS_EPS) * g


def _log_sigmoid(x):
    return jnp.minimum(x, 0.0) - jnp.log1p(jnp.exp(-jnp.abs(x)))


def _dot_exact01(a, m01):
    hi = a.astype(BF16)
    r1 = a - hi.astype(F32)
    mid = r1.astype(BF16)
    lo = (r1 - mid.astype(F32)).astype(BF16)
    out = jnp.dot(hi, m01, preferred_element_type=F32)
    out = out + jnp.dot(mid, m01, preferred_element_type=F32)
    return out + jnp.dot(lo, m01, preferred_element_type=F32)


def _mod_kernel(c_ref, w_ref, b_ref, o_ref):
    c = c_ref[...]
    a = (c * jax.nn.sigmoid(c)).astype(BF16)
    o_ref[...] = jnp.dot(a, w_ref[...].astype(BF16), preferred_element_type=F32) + b_ref[...]


def _modulation(c_all, w_ada, b_ada):
    depth, d, n_mod = w_ada.shape
    r = c_all.shape[0]
    tn = n_mod // 4
    return pl.pallas_call(
        _mod_kernel,
        grid=(depth, n_mod // tn),
        in_specs=[pl.BlockSpec((r, d), lambda l, j: (0, 0)),
                  pl.BlockSpec((None, d, tn), lambda l, j: (l, 0, j)),
                  pl.BlockSpec((None, 1, tn), lambda l, j: (l, 0, j))],
        out_specs=pl.BlockSpec((None, r, tn), lambda l, j: (l, 0, j)),
        out_shape=jax.ShapeDtypeStruct((depth, r, n_mod), F32),
        compiler_params=_params("arbitrary", "arbitrary"),
        name="adaln_mod",
    )(c_all, w_ada, b_ada.reshape(depth, 1, n_mod))


def _in_prompt_kernel(x_ref, mod_ref, g_ref, wq_ref, wkvT_ref, wfT_ref, bfT_ref, wu_ref, wgab_ref, tri_ref,
                      q_ref, kT_ref, vT_ref, kTb_ref, vTb_ref, lfT_ref, cT_ref, d_ref, ga_ref, gb_ref, ulast_ref,
                      carry_ref, ubuf_ref, *, tm):
    t = pl.program_id(1)
    d_model = x_ref.shape[-1]

    @pl.when(t == 0)
    def _():
        carry_ref[...] = jnp.zeros_like(carry_ref)
        ubuf_ref[0:HALO, :] = jnp.zeros((HALO, D_POOL), F32)

    h = _rms(x_ref[...], g_ref[...]) * (1.0 + mod_ref[1:2, :]) + mod_ref[0:1, :]
    hb = h.astype(BF16)

    q = jnp.dot(hb, wq_ref[...], preferred_element_type=F32)
    q_ref[...] = (q * QK_SCALE).astype(BF16)

    kvT = lax.dot_general(wkvT_ref[...], hb, _NT, preferred_element_type=F32)
    kT = kvT[:D_ATTN]
    vT = kvT[D_ATTN:]
    kT_ref[...] = kT
    vT_ref[...] = vT
    kTb_ref[...] = kT.astype(BF16)
    vTb_ref[...] = vT.astype(BF16)

    gab = jnp.dot(hb, wgab_ref[...], preferred_element_type=F32)
    ga_ref[...] = gab[:, :d_model]
    gb_ref[...] = gab[:, d_model:]

    fT = lax.dot_general(wfT_ref[...], hb, _NT, preferred_element_type=F32) + bfT_ref[...]
    lfT = _log_sigmoid(fT)
    lfT_ref[...] = lfT[0:N_HEADS]
    cT = _dot_exact01(lfT, tri_ref[...])[0:N_HEADS] + carry_ref[...]
    cT_ref[...] = cT
    carry_ref[...] = cT[:, tm - 1:tm]

    u = jnp.dot(hb, wu_ref[...], preferred_element_type=F32)
    ubuf_ref[HALO:HALO + tm, :] = u
    pos = t * tm + lax.broadcasted_iota(jnp.int32, (tm, 1), 0)
    for g, w in enumerate(POOL_WINDOWS):
        lo, hi = g * POOL_GROUP_DIM, (g + 1) * POOL_GROUP_DIM
        ug = u[:, lo:hi]
        s = ug
        for i in range(1, w):
            s = s + ubuf_ref[HALO - i:HALO - i + tm, lo:hi]
        inv = 1.0 / jnp.minimum(pos + 1, w).astype(F32)
        d_ref[:, lo:hi] = (s * inv - ug).astype(BF16)
    tail = u[tm - HALO:, :]
    ulast_ref[...] = tail
    ubuf_ref[0:HALO, :] = tail


def _in_prompt(x, mod, g, wq, wkvT, wfT, bfT, wu, wgab, tri, *, tm):
    b, t, d = x.shape
    nt = t // tm
    row = lambda n, dt: jax.ShapeDtypeStruct((b, t, n), dt)
    col = lambda n, dt: jax.ShapeDtypeStruct((b, n, t), dt)
    full = lambda a: pl.BlockSpec(a.shape, lambda i, j: (0,) * a.ndim)
    rspec = lambda n: pl.BlockSpec((None, tm, n), lambda i, j: (i, j, 0))
    cspec = lambda n: pl.BlockSpec((None, n, tm), lambda i, j: (i, 0, j))
    return pl.pallas_call(
        functools.partial(_in_prompt_kernel, tm=tm),
        grid=(b, nt),
        in_specs=[rspec(d), pl.BlockSpec((None, 6, d), lambda i, j: (i, 0, 0)), full(g), full(wq), full(wkvT),
                  full(wfT), full(bfT), full(wu), full(wgab), full(tri)],
        out_specs=[rspec(D_ATTN), cspec(D_ATTN), cspec(D_ATTN), cspec(D_ATTN), cspec(D_ATTN),
                   cspec(N_HEADS), cspec(N_HEADS), rspec(D_POOL), rspec(d), rspec(d),
                   pl.BlockSpec((None, HALO, D_POOL), lambda i, j: (i, 0, 0))],
        out_shape=[row(D_ATTN, BF16), col(D_ATTN, F32), col(D_ATTN, F32), col(D_ATTN, BF16), col(D_ATTN, BF16),
                   col(N_HEADS, F32), col(N_HEADS, F32), row(D_POOL, BF16), row(d, F32), row(d, F32),
                   jax.ShapeDtypeStruct((b, HALO, D_POOL), F32)],
        scratch_shapes=[pltpu.VMEM((N_HEADS, 1), F32), pltpu.VMEM((tm + HALO, D_POOL), F32)],
        compiler_params=_params("arbitrary", "arbitrary"),
        name="in_proj_prompt",
    )(x, mod, g, wq, wkvT, wfT, bfT, wu, wgab, tri)


def _in_sample_kernel(x_ref, mod_ref, g_ref, wq_ref, wkv_ref, wf_ref, bf_ref, wu_ref, wgab_ref,
                      q_ref, k_ref, v_ref, lf_ref, u_ref, ga_ref, gb_ref):
    d_model = x_ref.shape[-1]
    h = _rms(x_ref[...], g_ref[...]) * (1.0 + mod_ref[1]) + mod_ref[0]
    hb = h.astype(BF16)
    q_ref[...] = jnp.dot(hb, wq_ref[...], preferred_element_type=F32) * QK_SCALE
    kv = jnp.dot(hb, wkv_ref[...], preferred_element_type=F32)
    k_ref[...] = kv[:, :D_ATTN]
    v_ref[...] = kv[:, D_ATTN:]
    f = jnp.dot(hb, wf_ref[...], preferred_element_type=F32) + bf_ref[...]
    lf_ref[...] = _log_sigmoid(f)
    u_ref[...] = jnp.dot(hb, wu_ref[...], preferred_element_type=F32)
    gab = jnp.dot(hb, wgab_ref[...], preferred_element_type=F32)
    ga_ref[...] = gab[:, :d_model]
    gb_ref[...] = gab[:, d_model:]


def _in_sample(x, mod_rows, g, wq, wkv, wf, bf, wu, wgab):
    m, d = x.shape
    out = lambda n: jax.ShapeDtypeStruct((m, n), F32)
    return pl.pallas_call(
        _in_sample_kernel,
        out_shape=[out(D_ATTN), out(D_ATTN), out(D_ATTN), out(wf.shape[1]), out(D_POOL), out(d), out(d)],
        compiler_params=pltpu.CompilerParams(vmem_limit_bytes=VMEM_LIMIT),
        name="in_proj_sample",
    )(x, mod_rows, g, wq, wkv, wf, bf, wu, wgab)


def _attn_kernel(qi_ref, kj_ref, q_ref, kT_ref, vT_ref, cq_ref, ck_ref, o_ref, m_ref, l_ref, acc_ref, *, bq, bk):
    p = pl.program_id(1)
    qi = qi_ref[p]
    kj = kj_ref[p]

    @pl.when(kj == 0)
    def _():
        m_ref[...] = jnp.full(m_ref.shape, NEG, F32)
        l_ref[...] = jnp.zeros_like(l_ref)
        acc_ref[...] = jnp.zeros_like(acc_ref)

    def step(masked):
        if masked:
            row = qi * bq + lax.broadcasted_iota(jnp.int32, (bq, bk), 0)
            col = kj * bk + lax.broadcasted_iota(jnp.int32, (bq, bk), 1)
            keep = col <= row
        for h in range(N_HEADS):
            lo, hi = h * HEAD_DIM, (h + 1) * HEAD_DIM
            s = jnp.dot(q_ref[:, lo:hi], kT_ref[lo:hi, :], preferred_element_type=F32)
            s = s + (cq_ref[h:h + 1, 0:1] - ck_ref[h:h + 1, :])
            if masked:
                s = jnp.where(keep, s, NEG)
            m_old = m_ref[h]
            m_new = jnp.maximum(m_old, jnp.max(s, axis=1, keepdims=True))
            alpha = jnp.exp(m_old - m_new)
            pm = jnp.exp(s - m_new)
            l_ref[h] = alpha * l_ref[h] + jnp.sum(pm, axis=1, keepdims=True)
            pv = lax.dot_general(pm.astype(BF16), vT_ref[lo:hi, :], _NT, preferred_element_type=F32)
            acc_ref[h] = alpha * acc_ref[h] + pv
            m_ref[h] = m_new

    crosses_diag = (kj + 1) * bk - 1 > qi * bq
    pl.when(crosses_diag)(lambda: step(True))
    pl.when(jnp.logical_not(crosses_diag))(lambda: step(False))

    @pl.when(kj == ((qi + 1) * bq) // bk - 1)
    def _():
        for h in range(N_HEADS):
            o_ref[:, h * HEAD_DIM:(h + 1) * HEAD_DIM] = (acc_ref[h] / l_ref[h]).astype(BF16)


def _attention_prompt(q, kTb, vTb, cT, *, bq, bk):
    b, t, _ = q.shape
    nq = t // bq
    pairs = [(i, j) for i in range(nq) for j in range(((i + 1) * bq) // bk)]
    qi_tab = jnp.asarray(np.array([p[0] for p in pairs], np.int32))
    kj_tab = jnp.asarray(np.array([p[1] for p in pairs], np.int32))
    grid_spec = pltpu.PrefetchScalarGridSpec(
        num_scalar_prefetch=2,
        grid=(b, len(pairs)),
        in_specs=[pl.BlockSpec((None, bq, D_ATTN), lambda i, p, qt, kt: (i, qt[p], 0)),
                  pl.BlockSpec((None, D_ATTN, bk), lambda i, p, qt, kt: (i, 0, kt[p])),
                  pl.BlockSpec((None, D_ATTN, bk), lambda i, p, qt, kt: (i, 0, kt[p])),
                  pl.BlockSpec((None, N_HEADS, bq), lambda i, p, qt, kt: (i, 0, qt[p])),
                  pl.BlockSpec((None, N_HEADS, bk), lambda i, p, qt, kt: (i, 0, kt[p]))],
        out_specs=pl.BlockSpec((None, bq, D_ATTN), lambda i, p, qt, kt: (i, qt[p], 0)),
        scratch_shapes=[pltpu.VMEM((N_HEADS, bq, 1), F32), pltpu.VMEM((N_HEADS, bq, 1), F32),
                        pltpu.VMEM((N_HEADS, bq, HEAD_DIM), F32)],
    )
    return pl.pallas_call(
        functools.partial(_attn_kernel, bq=bq, bk=bk),
        grid_spec=grid_spec,
        out_shape=jax.ShapeDtypeStruct((b, t, D_ATTN), BF16),
        compiler_params=_params("arbitrary", "arbitrary"),
        name="fox_attn_prompt",
    )(qi_tab, kj_tab, q, kTb, vTb, cT, cT)


def _sattn_kernel(pt_ref, qbd_ref, knew_ref, vnew_ref, lfn_ref, mstrict_ref, *rest, n_pg, n_groups, n_new):
    k_refs = rest[:n_pg]
    v_refs = rest[n_pg:2 * n_pg]
    lf_refs = rest[2 * n_pg:3 * n_pg]
    o_ref, m_ref, l_ref, acc_ref, carry_ref = rest[3 * n_pg:]
    g = pl.program_id(1)
    rows = n_new * N_HEADS
    qbd = qbd_ref[...]
    tile_rows = lambda a: jnp.concatenate([a] * n_new, axis=0)

    @pl.when(g == 0)
    def _():
        qf = qbd.astype(F32)
        lfn = lfn_ref[...]
        r_idx = lax.broadcasted_iota(jnp.int32, (rows, 1), 0)
        s_cols = []
        run = jnp.zeros((N_HEADS, 1), F32)
        for tp in range(n_new):
            run = run + lfn[:, tp:tp + 1]
            s = jnp.sum(qf * knew_ref[tp:tp + 1, :], axis=1, keepdims=True) - tile_rows(run)
            s_cols.append(jnp.where(r_idx >= tp * N_HEADS, s, NEG))
        m = s_cols[0]
        for s in s_cols[1:]:
            m = jnp.maximum(m, s)
        l = jnp.zeros((rows, 1), F32)
        acc = jnp.zeros((rows, D_ATTN), F32)
        for tp, s in enumerate(s_cols):
            pm = jnp.exp(s - m)
            l = l + pm
            acc = acc + pm * vnew_ref[tp:tp + 1, :]
        m_ref[...] = m
        l_ref[...] = l
        acc_ref[...] = acc
        carry_ref[...] = jnp.zeros_like(carry_ref)

    for i in range(n_pg):
        lf = lf_refs[i][...]
        lf2 = jnp.concatenate([lf, lf], axis=0)
        suffix = _dot_exact01(lf2, mstrict_ref[...])[0:N_HEADS] + carry_ref[...]
        carry_ref[...] = suffix[:, 0:1] + lf[:, 0:1]
        s = jnp.dot(qbd, k_refs[i][...].astype(BF16), preferred_element_type=F32) + tile_rows(suffix)
        m_old = m_ref[...]
        m_new = jnp.maximum(m_old, jnp.max(s, axis=1, keepdims=True))
        alpha = jnp.exp(m_old - m_new)
        pm = jnp.exp(s - m_new)
        l_ref[...] = alpha * l_ref[...] + jnp.sum(pm, axis=1, keepdims=True)
        pv = lax.dot_general(pm.astype(BF16), v_refs[i][...].astype(BF16), _NT, preferred_element_type=F32)
        acc_ref[...] = alpha * acc_ref[...] + pv
        m_ref[...] = m_new

    @pl.when(g == n_groups - 1)
    def _():
        o = acc_ref[...] / l_ref[...]
        r_head = lax.broadcasted_iota(jnp.int32, (rows, D_ATTN), 0) % N_HEADS
        c_head = lax.broadcasted_iota(jnp.int32, (rows, D_ATTN), 1) // HEAD_DIM
        o = jnp.where(r_head == c_head, o, 0.0)
        for t in range(n_new):
            o_ref[t:t + 1, :] = jnp.sum(o[t * N_HEADS:(t + 1) * N_HEADS], axis=0, keepdims=True)


def _attention_sample(pt_flat, qbd, knew, vnew, lfnT, mstrict, ckT, cvT, clfT, *, layer, n_pages, n_pg):
    nb, rows, _ = qbd.shape
    n_new = knew.shape[1]
    page = ckT.shape[-1]
    n_groups = n_pages // n_pg

    def page_map(i):
        def f(b, g, pt):
            return (layer, pt[b * n_pages + (n_groups - 1 - g) * n_pg + (n_pg - 1 - i)], 0, 0)
        return f

    per_seq = lambda shape: pl.BlockSpec((None,) + shape, lambda b, g, pt: (b, 0, 0))
    in_specs = [per_seq((rows, D_ATTN)), per_seq((n_new, D_ATTN)), per_seq((n_new, D_ATTN)),
                per_seq((N_HEADS, n_new)), pl.BlockSpec(mstrict.shape, lambda b, g, pt: (0, 0))]
    in_specs += [pl.BlockSpec((None, None, D_ATTN, page), page_map(i)) for i in range(n_pg)]
    in_specs += [pl.BlockSpec((None, None, D_ATTN, page), page_map(i)) for i in range(n_pg)]
    in_specs += [pl.BlockSpec((None, None, N_HEADS, page), page_map(i)) for i in range(n_pg)]
    grid_spec = pltpu.PrefetchScalarGridSpec(
        num_scalar_prefetch=1,
        grid=(nb, n_groups),
        in_specs=in_specs,
        out_specs=per_seq((n_new, D_ATTN)),
        scratch_shapes=[pltpu.VMEM((rows, 1), F32), pltpu.VMEM((rows, 1), F32), pltpu.VMEM((rows, D_ATTN), F32),
                        pltpu.VMEM((N_HEADS, 1), F32)],
    )
    return pl.pallas_call(
        functools.partial(_sattn_kernel, n_pg=n_pg, n_groups=n_groups, n_new=n_new),
        grid_spec=grid_spec,
        out_shape=jax.ShapeDtypeStruct((nb, n_new, D_ATTN), F32),
        compiler_params=_params("arbitrary", "arbitrary"),
        name="fox_attn_sample",
    )(pt_flat, qbd, knew, vnew, lfnT, mstrict, *([ckT] * n_pg), *([cvT] * n_pg), *([clfT] * n_pg))


def _pool_sample_kernel(ext_ref, d_ref, *, n_new):
    for g, w in enumerate(POOL_WINDOWS):
        lo, hi = g * POOL_GROUP_DIM, (g + 1) * POOL_GROUP_DIM
        u = ext_ref[:, POOL_BUF:POOL_BUF + n_new, lo:hi]
        s = u
        for i in range(1, w):
            s = s + ext_ref[:, POOL_BUF - i:POOL_BUF - i + n_new, lo:hi]
        d_ref[:, :, lo:hi] = s * (1.0 / w) - u


def _pool_sample(ext, n_new):
    nb = ext.shape[0]
    return pl.pallas_call(
        functools.partial(_pool_sample_kernel, n_new=n_new),
        out_shape=jax.ShapeDtypeStruct((nb, n_new, D_POOL), F32),
        name="pool_sample",
    )(ext)


def _post_kernel(o_ref, d_ref, ga_ref, gb_ref, x_ref, mod_ref, wmix_ref, pscale_ref, wau_ref, wpu_ref, wout_ref,
                 g_ref, y_ref, *, per_row):
    d = d_ref[...].astype(BF16)
    parts = [jnp.dot(d[:, g * POOL_GROUP_DIM:(g + 1) * POOL_GROUP_DIM], wmix_ref[g], preferred_element_type=F32)
             for g in range(len(POOL_WINDOWS))]
    o_pool = jnp.concatenate(parts, axis=1) * pscale_ref[...]
    a = jnp.dot(o_ref[...].astype(BF16), wau_ref[...], preferred_element_type=F32)
    b = jnp.dot(o_pool.astype(BF16), wpu_ref[...], preferred_element_type=F32)
    merged = jax.nn.sigmoid(ga_ref[...]) * a + jax.nn.sigmoid(gb_ref[...]) * b
    y = jnp.dot(merged.astype(BF16), wout_ref[...], preferred_element_type=F32)
    gate = mod_ref[2] if per_row else mod_ref[2:3, :]
    y_ref[...] = x_ref[...] + gate * _rms(y, g_ref[...])


def _mod_spec(mod, per_row, tm):
    if per_row:
        return pl.BlockSpec((6, tm, mod.shape[-1]), lambda i, j: (0, j, 0))
    return pl.BlockSpec((None, 6, mod.shape[-1]), lambda i, j: (i, 0, 0))


def _post(o, dpool, ga, gb, x, mod, wmix, pscale, wau, wpu, wout, g, *, tm, per_row):
    b, t, d = x.shape
    full = lambda a: pl.BlockSpec(a.shape, lambda i, j: (0,) * a.ndim)
    rspec = lambda n: pl.BlockSpec((None, tm, n), lambda i, j: (i, j, 0))
    return pl.pallas_call(
        functools.partial(_post_kernel, per_row=per_row),
        grid=(b, t // tm),
        in_specs=[rspec(D_ATTN), rspec(D_POOL), rspec(d), rspec(d), rspec(d), _mod_spec(mod, per_row, tm),
                  full(wmix), full(pscale), full(wau), full(wpu), full(wout), full(g)],
        out_specs=rspec(d),
        out_shape=jax.ShapeDtypeStruct((b, t, d), F32),
        compiler_params=_params("arbitrary", "arbitrary"),
        name="post_mix",
    )(o, dpool, ga, gb, x, mod, wmix, pscale, wau, wpu, wout, g)


def _ffn_kernel(x_ref, mod_ref, g1_ref, w1_ref, w2_ref, g2_ref, y_ref, *, per_row, fc):
    x = x_ref[...]
    if per_row:
        shift, scale, gate = mod_ref[3], mod_ref[4], mod_ref[5]
    else:
        shift, scale, gate = mod_ref[3:4, :], mod_ref[4:5, :], mod_ref[5:6, :]
    hb = (_rms(x, g1_ref[...]) * (1.0 + scale) + shift).astype(BF16)
    acc = jnp.zeros(x.shape, F32)
    for c in range(w1_ref.shape[1] // fc):
        a = jnp.dot(hb, w1_ref[:, c * fc:(c + 1) * fc], preferred_element_type=F32)
        a = jnp.square(jnp.maximum(a, 0.0))
        acc = acc + jnp.dot(a.astype(BF16), w2_ref[c * fc:(c + 1) * fc, :], preferred_element_type=F32)
    y_ref[...] = x + gate * _rms(acc, g2_ref[...])


def _ffn(x, mod, g1, w1, w2, g2, *, tm, per_row):
    b, t, d = x.shape
    full = lambda a: pl.BlockSpec(a.shape, lambda i, j: (0,) * a.ndim)
    rspec = lambda n: pl.BlockSpec((None, tm, n), lambda i, j: (i, j, 0))
    return pl.pallas_call(
        functools.partial(_ffn_kernel, per_row=per_row, fc=min(1024, w1.shape[1])),
        grid=(b, t // tm),
        in_specs=[rspec(d), _mod_spec(mod, per_row, tm), full(g1), full(w1), full(w2), full(g2)],
        out_specs=rspec(d),
        out_shape=jax.ShapeDtypeStruct((b, t, d), F32),
        compiler_params=_params("arbitrary", "arbitrary"),
        name="ffn",
    )(x, mod, g1, w1, w2, g2)


def kernel(x_prompt, x_sample, cache_k, cache_v, cache_logf, state_pool, page_table, c_prompt, c_sample, w_ada, b_ada, g_pre_mix, w_in, b_forget, w_pool_mix, pool_scale, w_attn_up, w_pool_up, w_out, g_post_mix, g_pre_ffn, w_ff1, w_ff2, g_post_ffn):
    depth = w_in.shape[0]
    nb_p, t_p, d = x_prompt.shape
    nb_s, t_s, _ = x_sample.shape
    m_s = nb_s * t_s
    n_phys, page = cache_k.shape[1], cache_k.shape[2]
    n_pages = page_table.shape[1]
    assert t_p % HALO == 0 and t_p >= HALO and t_s <= POOL_BUF

    tm = min(512, t_p)
    bq = bk = min(512, t_p)
    n_pg = min(8, n_pages)
    assert t_p % tm == 0 and n_pages % n_pg == 0

    n_c = nb_p + nb_s
    pad = (-n_c) % 8
    c_all = jnp.concatenate([c_prompt, c_sample, jnp.zeros((pad, d), F32)], axis=0)
    mod_all = _modulation(c_all, w_ada, b_ada)

    ckT = cache_k.transpose(0, 1, 3, 4, 2).reshape(depth, n_phys, D_ATTN, page)
    cvT = cache_v.transpose(0, 1, 3, 4, 2).reshape(depth, n_phys, D_ATTN, page)
    clfT = cache_logf.transpose(0, 1, 3, 2)
    pt_flat = page_table.reshape(-1).astype(jnp.int32)

    ii = np.arange(tm)
    tri = jnp.asarray((ii[:, None] <= ii[None, :]).astype(np.float32), BF16)
    jj = np.arange(page)
    mstrict = jnp.asarray((jj[:, None] > jj[None, :]).astype(np.float32), BF16)
    head_of_col = np.arange(D_ATTN) // HEAD_DIM
    bd_mask = jnp.asarray(head_of_col[None, :] == (np.arange(t_s * N_HEADS) % N_HEADS)[:, None])

    o_q, o_k, o_v, o_f, o_u, o_ga = D_ATTN, 2 * D_ATTN, 3 * D_ATTN, 3 * D_ATTN + N_HEADS, 3 * D_ATTN + N_HEADS + D_POOL, 3 * D_ATTN + N_HEADS + D_POOL + d

    xp = x_prompt
    xs = x_sample.reshape(1, m_s, d)
    kp, vp, lp, pp, ks, vs, ls, ps = [], [], [], [], [], [], [], []
    for l in range(depth):
        wl = w_in[l]
        wq = wl[:, :o_q].astype(BF16)
        wkv = wl[:, o_q:o_v].astype(BF16)
        wkvT = wl[:, o_q:o_v].T.astype(BF16)
        wf = wl[:, o_v:o_f]
        wfT = jnp.pad(wf.T, ((0, 16 - N_HEADS), (0, 0))).astype(BF16)
        wf_pad = jnp.pad(wf, ((0, 0), (0, 128 - N_HEADS))).astype(BF16)
        bfT = jnp.pad(b_forget[l].reshape(N_HEADS, 1), ((0, 16 - N_HEADS), (0, 0)))
        bf_pad = jnp.pad(b_forget[l].reshape(1, N_HEADS), ((0, 0), (0, 128 - N_HEADS)))
        wu = wl[:, o_f:o_u].astype(BF16)
        wgab = wl[:, o_u:].astype(BF16)
        wmix = w_pool_mix[l].astype(BF16)
        pscale = pool_scale[l].reshape(1, D_POOL)
        wau = w_attn_up[l].astype(BF16)
        wpu = w_pool_up[l].astype(BF16)
        wout = w_out[l].astype(BF16)
        w1 = w_ff1[l].astype(BF16)
        w2 = w_ff2[l].astype(BF16)
        g_pm = g_pre_mix[l].reshape(1, d)
        g_qm = g_post_mix[l].reshape(1, d)
        g_pf = g_pre_ffn[l].reshape(1, d)
        g_qf = g_post_ffn[l].reshape(1, d)

        mod_p = mod_all[l, :nb_p].reshape(nb_p, 6, d)
        mod_s = jnp.repeat(mod_all[l, nb_p:n_c].reshape(nb_s, 6, d), t_s, axis=0).transpose(1, 0, 2)

        q, kT, vT, kTb, vTb, lfT, cT, dpool, ga, gb, ulast = _in_prompt(
            xp, mod_p, g_pm, wq, wkvT, wfT, bfT, wu, wgab, tri, tm=tm)
        o_attn = _attention_prompt(q, kTb, vTb, cT, bq=bq, bk=bk)
        xp = _post(o_attn, dpool, ga, gb, xp, mod_p, wmix, pscale, wau, wpu, wout, g_qm, tm=tm, per_row=False)
        xp = _ffn(xp, mod_p, g_pf, w1, w2, g_qf, tm=tm, per_row=False)
        kp.append(kT)
        vp.append(vT)
        lp.append(lfT)
        pp.append(ulast[:, HALO - POOL_BUF:, :])

        q_s, k_s, v_s, lf_s, u_s, ga_s, gb_s = _in_sample(xs[0], mod_s, g_pm, wq, wkv, wf_pad, bf_pad, wu, wgab)
        lf_s = lf_s[:, :N_HEADS].reshape(nb_s, t_s, N_HEADS)
        ext = jnp.concatenate([state_pool[l], u_s.reshape(nb_s, t_s, D_POOL)], axis=1)
        d_s = _pool_sample(ext, t_s)
        qbd = jnp.where(bd_mask[None], jnp.repeat(q_s.reshape(nb_s, t_s, 1, D_ATTN), N_HEADS, axis=2)
                        .reshape(nb_s, t_s * N_HEADS, D_ATTN), 0.0).astype(BF16)
        o_s = _attention_sample(pt_flat, qbd, k_s.reshape(nb_s, t_s, D_ATTN), v_s.reshape(nb_s, t_s, D_ATTN),
                                lf_s.transpose(0, 2, 1), mstrict, ckT, cvT, clfT,
                                layer=l, n_pages=n_pages, n_pg=n_pg)
        xs = _post(o_s.reshape(1, m_s, D_ATTN), d_s.reshape(1, m_s, D_POOL), ga_s[None], gb_s[None], xs, mod_s,
                   wmix, pscale, wau, wpu, wout, g_qm, tm=m_s, per_row=True)
        xs = _ffn(xs, mod_s, g_pf, w1, w2, g_qf, tm=m_s, per_row=True)
        ks.append(k_s.reshape(nb_s, t_s, N_HEADS, HEAD_DIM))
        vs.append(v_s.reshape(nb_s, t_s, N_HEADS, HEAD_DIM))
        ls.append(lf_s)
        ps.append(ext[:, t_s:, :])

    def heads_last(a):
        return a.reshape(depth, nb_p, N_HEADS, HEAD_DIM, t_p).transpose(0, 1, 4, 2, 3)

    return (xp, xs.reshape(nb_s, t_s, d),
            heads_last(jnp.stack(kp)), heads_last(jnp.stack(vp)), jnp.stack(lp).transpose(0, 1, 3, 2), jnp.stack(pp),
            jnp.stack(ks), jnp.stack(vs), jnp.stack(ls), jnp.stack(ps))
```

```python
import functools

import numpy as np
import jax
import jax.numpy as jnp
from jax import lax
from jax.experimental import pallas as pl
from jax.experimental.pallas import tpu as pltpu

F32 = jnp.float32
BF16 = jnp.bfloat16

N_HEADS = 8
HEAD_DIM = 64
D_ATTN = N_HEADS * HEAD_DIM
POOL_WINDOWS = (2, 4, 8, 16)
POOL_GROUP_DIM = 128
D_POOL = len(POOL_WINDOWS) * POOL_GROUP_DIM
POOL_BUF = max(POOL_WINDOWS) - 1
HALO = 16
RMS_EPS = 1e-6
NEG = -1e30
QK_SCALE = HEAD_DIM ** -0.5
LOG2E = 1.4426950408889634

LANES = 128
SLAB = LANES
N_PIECES = 3
V_ROWS = HEAD_DIM + 16
ONES_SLOT = N_PIECES * N_HEADS

VMEM_LIMIT = 48 * 1024 * 1024

_NT = (((1,), (1,)), ((), ()))
_TN = (((0,), (0,)), ((), ()))


def _params(*sem, flags=None):
    return pltpu.CompilerParams(dimension_semantics=sem, vmem_limit_bytes=VMEM_LIMIT, flags=flags)


def _rms(x, g):
    return x * lax.rsqrt(jnp.mean(x * x, axis=-1, keepdims=True) + RMS_EPS) * g


def _log_sigmoid(x):
    return jnp.minimum(x, 0.0) - jnp.log1p(jnp.exp(-jnp.abs(x)))


def _split3(a):
    hi = a.astype(BF16).astype(F32)
    r1 = a - hi
    mid = r1.astype(BF16).astype(F32)
    lo = (r1 - mid).astype(BF16).astype(F32)
    return hi, mid, lo


def _dot_exact01(a, m01):
    return sum(jnp.dot(p.astype(BF16), m01, preferred_element_type=F32) for p in _split3(a))


def _dot_exact01_left(m01, a):
    return sum(jnp.dot(m01, p.astype(BF16), preferred_element_type=F32) for p in _split3(a))


def _mod_kernel(c_ref, w_ref, b_ref, o_ref):
    c = c_ref[...]
    a = (c * jax.nn.sigmoid(c)).astype(BF16)
    o_ref[...] = jnp.dot(a, w_ref[...].astype(BF16), preferred_element_type=F32) + b_ref[...]


def _modulation(c_all, w_ada, b_ada):
    depth, d, n_mod = w_ada.shape
    r = c_all.shape[0]
    tn = n_mod // 4
    return pl.pallas_call(
        _mod_kernel,
        grid=(depth, n_mod // tn),
        in_specs=[pl.BlockSpec((r, d), lambda l, j: (0, 0)),
                  pl.BlockSpec((None, d, tn), lambda l, j: (l, 0, j)),
                  pl.BlockSpec((None, 1, tn), lambda l, j: (l, 0, j))],
        out_specs=pl.BlockSpec((None, r, tn), lambda l, j: (l, 0, j)),
        out_shape=jax.ShapeDtypeStruct((depth, r, n_mod), F32),
        compiler_params=_params("arbitrary", "arbitrary"),
        name="adaln_mod",
    )(c_all, w_ada, b_ada.reshape(depth, 1, n_mod))


def _in_prompt_kernel(x_ref, mod_ref, g_ref, wqaT_ref, wkT_ref, wka_ref, wvaT_ref, wf_ref, bf_ref, wfT_ref, bfT_ref,
                      wu_ref, wgab_ref, triu_ref, tril_ref, pk_ref, pqT_ref, vones_ref, *rest, tm, n_prev):
    (qaT_ref, ka_ref, vaT_ref, kT_ref, vT_ref, lfT_ref, d_ref, ga_ref, gb_ref, ulast_ref,
     crow_ref, ccol_ref, ubuf_ref) = rest[n_prev:]
    t = pl.program_id(1)
    d_model = x_ref.shape[-1]

    @pl.when(t == 0)
    def _():
        crow_ref[...] = jnp.zeros_like(crow_ref)
        ccol_ref[...] = jnp.zeros_like(ccol_ref)
        ubuf_ref[0:HALO, :] = jnp.zeros((HALO, D_POOL), F32)

    h = _rms(x_ref[...], g_ref[...]) * (1.0 + mod_ref[1:2, :]) + mod_ref[0:1, :]
    hb = h.astype(BF16)

    lf_rows = _log_sigmoid(jnp.dot(hb, wf_ref[...], preferred_element_type=F32) + bf_ref[...])
    c_rows = _dot_exact01_left(tril_ref[...], lf_rows) + crow_ref[...]
    crow_ref[...] = c_rows[tm - 1:tm, :]
    lane = lax.broadcasted_iota(jnp.int32, c_rows.shape, 1)
    hi, mid, lo = _split3(c_rows * LOG2E)
    cp = jnp.where(lane < N_HEADS, hi, jnp.where(lane < 2 * N_HEADS, mid, jnp.where(lane < ONES_SLOT, lo, 1.0)))

    lfT = _log_sigmoid(lax.dot_general(wfT_ref[...], hb, _NT, preferred_element_type=F32) + bfT_ref[...])
    lfT_ref[...] = lfT[0:N_HEADS]
    cT = _dot_exact01(lfT, triu_ref[...]) + ccol_ref[...]
    ccol_ref[...] = cT[:, tm - 1:tm]
    row = lax.broadcasted_iota(jnp.int32, cT.shape, 0)
    hiT, midT, loT = _split3(cT * LOG2E)
    cpT = jnp.where(row < N_HEADS, hiT, jnp.where(row < 2 * N_HEADS, midT, jnp.where(row < ONES_SLOT, loT, 1.0)))
    cpT = jnp.concatenate([cpT, jnp.zeros((LANES - cpT.shape[0], tm), F32)], axis=0).astype(BF16)

    ka = jnp.dot(hb, wka_ref[...], preferred_element_type=F32)
    ka = ka + jnp.dot(cp.astype(BF16), pk_ref[...], preferred_element_type=F32)
    ka_ref[...] = ka.astype(BF16)
    qaT = lax.dot_general(wqaT_ref[...], hb, _NT, preferred_element_type=F32)
    qaT = qaT + jnp.dot(pqT_ref[...], cpT, preferred_element_type=F32)
    qaT_ref[...] = qaT.astype(BF16)
    vaT = lax.dot_general(wvaT_ref[...], hb, _NT, preferred_element_type=F32)
    for hd in range(N_HEADS):
        vT_ref[hd * HEAD_DIM:(hd + 1) * HEAD_DIM, :] = vaT[hd * V_ROWS:hd * V_ROWS + HEAD_DIM, :]
    vaT_ref[...] = (vaT + vones_ref[...]).astype(BF16)
    kT_ref[...] = lax.dot_general(wkT_ref[...], hb, _NT, preferred_element_type=F32)

    gab = jnp.dot(hb, wgab_ref[...], preferred_element_type=F32)
    ga_ref[...] = gab[:, :d_model]
    gb_ref[...] = gab[:, d_model:]

    u = jnp.dot(hb, wu_ref[...], preferred_element_type=F32)
    ubuf_ref[HALO:HALO + tm, :] = u
    pos = t * tm + lax.broadcasted_iota(jnp.int32, (tm, 1), 0)
    for g, w in enumerate(POOL_WINDOWS):
        lo_c, hi_c = g * POOL_GROUP_DIM, (g + 1) * POOL_GROUP_DIM
        ug = u[:, lo_c:hi_c]
        s = ug
        for i in range(1, w):
            s = s + ubuf_ref[HALO - i:HALO - i + tm, lo_c:hi_c]
        inv = 1.0 / jnp.minimum(pos + 1, w).astype(F32)
        d_ref[:, lo_c:hi_c] = (s * inv - ug).astype(BF16)
    tail = u[tm - HALO:, :]
    ulast_ref[...] = tail
    ubuf_ref[0:HALO, :] = tail


def _in_prompt(x, mod, consts, prev, *, layer, depth, tm):
    b, t, d = x.shape
    nt = t // tm
    row = lambda n, dt: jax.ShapeDtypeStruct((b, t, n), dt)
    col = lambda n, dt: jax.ShapeDtypeStruct((b, n, t), dt)
    lay = lambda n: jax.ShapeDtypeStruct((depth, b, n, t), F32)
    full = lambda a: pl.BlockSpec(a.shape, lambda i, j: (0,) * a.ndim)
    rspec = lambda n: pl.BlockSpec((None, tm, n), lambda i, j: (i, j, 0))
    cspec = lambda n: pl.BlockSpec((None, n, tm), lambda i, j: (i, 0, j))
    lspec = lambda n: pl.BlockSpec((None, None, n, tm), lambda i, j: (layer, i, 0, j))
    prev = () if prev is None else tuple(prev)
    n_in = 2 + len(consts)
    return pl.pallas_call(
        functools.partial(_in_prompt_kernel, tm=tm, n_prev=len(prev)),
        grid=(b, nt),
        in_specs=[rspec(d), pl.BlockSpec((None, 6, d), lambda i, j: (i, 0, 0))] + [full(c) for c in consts]
                 + [pl.BlockSpec(memory_space=pl.ANY)] * len(prev),
        out_specs=[cspec(N_HEADS * SLAB), rspec(N_HEADS * SLAB), cspec(N_HEADS * V_ROWS),
                   lspec(D_ATTN), lspec(D_ATTN), lspec(N_HEADS), rspec(D_POOL), rspec(d), rspec(d),
                   pl.BlockSpec((None, HALO, D_POOL), lambda i, j: (i, 0, 0))],
        out_shape=[col(N_HEADS * SLAB, BF16), row(N_HEADS * SLAB, BF16), col(N_HEADS * V_ROWS, BF16),
                   lay(D_ATTN), lay(D_ATTN), lay(N_HEADS), row(D_POOL, BF16), row(d, F32), row(d, F32),
                   jax.ShapeDtypeStruct((b, HALO, D_POOL), F32)],
        scratch_shapes=[pltpu.VMEM((1, LANES), F32), pltpu.VMEM((4 * N_HEADS, 1), F32),
                        pltpu.VMEM((tm + HALO, D_POOL), F32)],
        input_output_aliases={n_in + i: 3 + i for i in range(len(prev))},
        compiler_params=_params("arbitrary", "arbitrary"),
        name="in_proj_prompt",
    )(x, mod, *consts, *prev)


def _in_sample_kernel(x_ref, mod_ref, g_ref, wq_ref, wkv_ref, wf_ref, bf_ref, wu_ref, wgab_ref,
                      q_ref, k_ref, v_ref, lf_ref, u_ref, ga_ref, gb_ref):
    d_model = x_ref.shape[-1]
    h = _rms(x_ref[...], g_ref[...]) * (1.0 + mod_ref[1]) + mod_ref[0]
    hb = h.astype(BF16)
    q_ref[...] = jnp.dot(hb, wq_ref[...], preferred_element_type=F32)
    kv = jnp.dot(hb, wkv_ref[...], preferred_element_type=F32)
    k_ref[...] = kv[:, :D_ATTN]
    v_ref[...] = kv[:, D_ATTN:]
    f = jnp.dot(hb, wf_ref[...], preferred_element_type=F32) + bf_ref[...]
    lf_ref[...] = _log_sigmoid(f)
    u_ref[...] = jnp.dot(hb, wu_ref[...], preferred_element_type=F32)
    gab = jnp.dot(hb, wgab_ref[...], preferred_element_type=F32)
    ga_ref[...] = gab[:, :d_model]
    gb_ref[...] = gab[:, d_model:]


def _in_sample(x, mod_rows, g, wq, wkv, wf, bf, wu, wgab):
    m, d = x.shape
    out = lambda n: jax.ShapeDtypeStruct((m, n), F32)
    return pl.pallas_call(
        _in_sample_kernel,
        out_shape=[out(D_ATTN), out(D_ATTN), out(D_ATTN), out(wf.shape[1]), out(D_POOL), out(d), out(d)],
        compiler_params=pltpu.CompilerParams(vmem_limit_bytes=VMEM_LIMIT),
        name="in_proj_sample",
    )(x, mod_rows, g, wq, wkv, wf, bf, wu, wgab)


def _attn_kernel(qi_ref, kj_ref, qaT_ref, ka_ref, vaT_ref, oT_ref, m_ref, acc_ref, *, bq, bk):
    p = pl.program_id(1)
    qi = qi_ref[p]
    kj = kj_ref[p]

    @pl.when(kj == 0)
    def _():
        m_ref[...] = jnp.full(m_ref.shape, NEG, F32)
        acc_ref[...] = jnp.zeros_like(acc_ref)

    def step(masked):
        if masked:
            key = kj * bk + lax.broadcasted_iota(jnp.int32, (bk, bq), 0)
            qry = qi * bq + lax.broadcasted_iota(jnp.int32, (bk, bq), 1)
            keep = key <= qry
        scores = lambda h: jnp.dot(ka_ref[:, h * SLAB:(h + 1) * SLAB], qaT_ref[h * SLAB:(h + 1) * SLAB, :],
                                   preferred_element_type=F32)

        def softmax(h, sT):
            if masked:
                sT = jnp.where(keep, sT, NEG)
            m_old = m_ref[h:h + 1, :]
            m_new = jnp.maximum(m_old, jnp.max(sT, axis=0, keepdims=True))
            m_ref[h:h + 1, :] = m_new
            return jnp.exp2(m_old - m_new), jnp.exp2(sT - m_new).astype(BF16)

        def accumulate(h, alpha, pT):
            pv = jnp.dot(vaT_ref[h * V_ROWS:(h + 1) * V_ROWS, :], pT, preferred_element_type=F32)
            acc_ref[h] = alpha * acc_ref[h] + pv

        s_cur = scores(0)
        pending = None
        for h in range(N_HEADS):
            s_nxt = scores(h + 1) if h + 1 < N_HEADS else None
            soft = softmax(h, s_cur)
            if pending is not None:
                accumulate(h - 1, *pending)
            pending = soft
            s_cur = s_nxt
        accumulate(N_HEADS - 1, *pending)

    crosses_diag = (kj + 1) * bk - 1 > qi * bq
    pl.when(crosses_diag)(lambda: step(True))
    pl.when(jnp.logical_not(crosses_diag))(lambda: step(False))

    @pl.when(kj == ((qi + 1) * bq) // bk - 1)
    def _():
        for h in range(N_HEADS):
            a = acc_ref[h]
            oT_ref[h * HEAD_DIM:(h + 1) * HEAD_DIM, :] = (a[0:HEAD_DIM] / a[HEAD_DIM:HEAD_DIM + 1]).astype(BF16)


def _attention_prompt(qaT, ka, vaT, *, bq, bk):
    b, t, _ = ka.shape
    nq = t // bq
    pairs = [(i, j) for i in range(nq) for j in range(((i + 1) * bq) // bk)]
    qi_tab = jnp.asarray(np.array([p[0] for p in pairs], np.int32))
    kj_tab = jnp.asarray(np.array([p[1] for p in pairs], np.int32))
    grid_spec = pltpu.PrefetchScalarGridSpec(
        num_scalar_prefetch=2,
        grid=(b, len(pairs)),
        in_specs=[pl.BlockSpec((None, N_HEADS * SLAB, bq), lambda i, p, qt, kt: (i, 0, qt[p])),
                  pl.BlockSpec((None, bk, N_HEADS * SLAB), lambda i, p, qt, kt: (i, kt[p], 0)),
                  pl.BlockSpec((None, N_HEADS * V_ROWS, bk), lambda i, p, qt, kt: (i, 0, kt[p]))],
        out_specs=pl.BlockSpec((None, D_ATTN, bq), lambda i, p, qt, kt: (i, 0, qt[p])),
        scratch_shapes=[pltpu.VMEM((N_HEADS, bq), F32), pltpu.VMEM((N_HEADS, V_ROWS, bq), F32)],
    )
    return pl.pallas_call(
        functools.partial(_attn_kernel, bq=bq, bk=bk),
        grid_spec=grid_spec,
        out_shape=jax.ShapeDtypeStruct((b, D_ATTN, t), BF16),
        compiler_params=_params("arbitrary", "arbitrary"),
        name="fox_attn_prompt",
    )(qi_tab, kj_tab, qaT, ka, vaT)


def _sattn_kernel(pt_ref, qbd_ref, knew_ref, vnew_ref, lfn_ref, mstrict_ref, *rest, n_pg, n_groups, n_new):
    k_refs = rest[:n_pg]
    v_refs = rest[n_pg:2 * n_pg]
    lf_refs = rest[2 * n_pg:3 * n_pg]
    o_ref, m_ref, l_ref, acc_ref, carry_ref = rest[3 * n_pg:]
    g = pl.program_id(1)
    rows = n_new * N_HEADS
    qbd = qbd_ref[...]
    tile_rows = lambda a: jnp.concatenate([a] * n_new, axis=0)

    @pl.when(g == 0)
    def _():
        qf = qbd.astype(F32)
        lfn = lfn_ref[...]
        r_idx = lax.broadcasted_iota(jnp.int32, (rows, 1), 0)
        s_cols = []
        run = jnp.zeros((N_HEADS, 1), F32)
        for tp in range(n_new):
            run = run + lfn[:, tp:tp + 1]
            s = jnp.sum(qf * knew_ref[tp:tp + 1, :], axis=1, keepdims=True) - tile_rows(run)
            s_cols.append(jnp.where(r_idx >= tp * N_HEADS, s, NEG))
        m = s_cols[0]
        for s in s_cols[1:]:
            m = jnp.maximum(m, s)
        l = jnp.zeros((rows, 1), F32)
        acc = jnp.zeros((rows, D_ATTN), F32)
        for tp, s in enumerate(s_cols):
            pm = jnp.exp(s - m)
            l = l + pm
            acc = acc + pm * vnew_ref[tp:tp + 1, :]
        m_ref[...] = m
        l_ref[...] = l
        acc_ref[...] = acc
        carry_ref[...] = jnp.zeros_like(carry_ref)

    carry = carry_ref[...]
    s_parts = []
    for i in range(n_pg):
        lf = lf_refs[i][...]
        lf2 = jnp.concatenate([lf, lf], axis=0)
        suffix = _dot_exact01(lf2, mstrict_ref[...])[0:N_HEADS] + carry
        carry = suffix[:, 0:1] + lf[:, 0:1]
        s_parts.append(jnp.dot(qbd, k_refs[i][...].astype(BF16), preferred_element_type=F32) + tile_rows(suffix))
    carry_ref[...] = carry
    s = jnp.concatenate(s_parts, axis=1)
    m_old = m_ref[...]
    m_new = jnp.maximum(m_old, jnp.max(s, axis=1, keepdims=True))
    alpha = jnp.exp(m_old - m_new)
    pm = jnp.exp(s - m_new)
    l_ref[...] = alpha * l_ref[...] + jnp.sum(pm, axis=1, keepdims=True)
    pmb = pm.astype(BF16)
    page = s_parts[0].shape[1]
    pv = jnp.zeros(acc_ref.shape, F32)
    for i in range(n_pg):
        pv = pv + lax.dot_general(pmb[:, i * page:(i + 1) * page], v_refs[i][...].astype(BF16), _NT,
                                  preferred_element_type=F32)
    acc_ref[...] = alpha * acc_ref[...] + pv
    m_ref[...] = m_new

    @pl.when(g == n_groups - 1)
    def _():
        o = acc_ref[...] / l_ref[...]
        r_head = lax.broadcasted_iota(jnp.int32, (rows, D_ATTN), 0) % N_HEADS
        c_head = lax.broadcasted_iota(jnp.int32, (rows, D_ATTN), 1) // HEAD_DIM
        o = jnp.where(r_head == c_head, o, 0.0)
        for t in range(n_new):
            o_ref[t:t + 1, :] = jnp.sum(o[t * N_HEADS:(t + 1) * N_HEADS], axis=0, keepdims=True)


def _attention_sample(pt_flat, qbd, knew, vnew, lfnT, mstrict, ckT, cvT, clfT, *, layer, n_pages, n_pg):
    nb, rows, _ = qbd.shape
    n_new = knew.shape[1]
    page = ckT.shape[-1]
    n_groups = n_pages // n_pg

    def page_map(i):
        def f(b, g, pt):
            return (layer, pt[b * n_pages + (n_groups - 1 - g) * n_pg + (n_pg - 1 - i)], 0, 0)
        return f

    per_seq = lambda shape: pl.BlockSpec((None,) + shape, lambda b, g, pt: (b, 0, 0))
    in_specs = [per_seq((rows, D_ATTN)), per_seq((n_new, D_ATTN)), per_seq((n_new, D_ATTN)),
                per_seq((N_HEADS, n_new)), pl.BlockSpec(mstrict.shape, lambda b, g, pt: (0, 0))]
    in_specs += [pl.BlockSpec((None, None, D_ATTN, page), page_map(i)) for i in range(n_pg)]
    in_specs += [pl.BlockSpec((None, None, D_ATTN, page), page_map(i)) for i in range(n_pg)]
    in_specs += [pl.BlockSpec((None, None, N_HEADS, page), page_map(i)) for i in range(n_pg)]
    grid_spec = pltpu.PrefetchScalarGridSpec(
        num_scalar_prefetch=1,
        grid=(nb, n_groups),
        in_specs=in_specs,
        out_specs=per_seq((n_new, D_ATTN)),
        scratch_shapes=[pltpu.VMEM((rows, 1), F32), pltpu.VMEM((rows, 1), F32), pltpu.VMEM((rows, D_ATTN), F32),
                        pltpu.VMEM((N_HEADS, 1), F32)],
    )
    return pl.pallas_call(
        functools.partial(_sattn_kernel, n_pg=n_pg, n_groups=n_groups, n_new=n_new),
        grid_spec=grid_spec,
        out_shape=jax.ShapeDtypeStruct((nb, n_new, D_ATTN), F32),
        compiler_params=_params("arbitrary", "arbitrary"),
        name="fox_attn_sample",
    )(pt_flat, qbd, knew, vnew, lfnT, mstrict, *([ckT] * n_pg), *([cvT] * n_pg), *([clfT] * n_pg))


def _pool_sample_kernel(ext_ref, d_ref, *, n_new):
    for g, w in enumerate(POOL_WINDOWS):
        lo, hi = g * POOL_GROUP_DIM, (g + 1) * POOL_GROUP_DIM
        u = ext_ref[:, POOL_BUF:POOL_BUF + n_new, lo:hi]
        s = u
        for i in range(1, w):
            s = s + ext_ref[:, POOL_BUF - i:POOL_BUF - i + n_new, lo:hi]
        d_ref[:, :, lo:hi] = s * (1.0 / w) - u


def _pool_sample(ext, n_new):
    nb = ext.shape[0]
    return pl.pallas_call(
        functools.partial(_pool_sample_kernel, n_new=n_new),
        out_shape=jax.ShapeDtypeStruct((nb, n_new, D_POOL), F32),
        name="pool_sample",
    )(ext)


def _post_kernel(o_ref, d_ref, ga_ref, gb_ref, x_ref, mod_ref, wmix_ref, pscale_ref, wau_ref, wpu_ref, wout_ref,
                 g_ref, y_ref, *, per_row, o_transposed):
    d = d_ref[...].astype(BF16)
    parts = [jnp.dot(d[:, g * POOL_GROUP_DIM:(g + 1) * POOL_GROUP_DIM], wmix_ref[g], preferred_element_type=F32)
             for g in range(len(POOL_WINDOWS))]
    o_pool = jnp.concatenate(parts, axis=1) * pscale_ref[...]
    if o_transposed:
        a = lax.dot_general(o_ref[...], wau_ref[...], _TN, preferred_element_type=F32)
    else:
        a = jnp.dot(o_ref[...].astype(BF16), wau_ref[...], preferred_element_type=F32)
    b = jnp.dot(o_pool.astype(BF16), wpu_ref[...], preferred_element_type=F32)
    merged = jax.nn.sigmoid(ga_ref[...]) * a + jax.nn.sigmoid(gb_ref[...]) * b
    y = jnp.dot(merged.astype(BF16), wout_ref[...], preferred_element_type=F32)
    gate = mod_ref[2] if per_row else mod_ref[2:3, :]
    y_ref[...] = x_ref[...] + gate * _rms(y, g_ref[...])


def _mod_spec(mod, per_row, tm):
    if per_row:
        return pl.BlockSpec((6, tm, mod.shape[-1]), lambda i, j: (0, j, 0))
    return pl.BlockSpec((None, 6, mod.shape[-1]), lambda i, j: (i, 0, 0))


def _post(o, dpool, ga, gb, x, mod, wmix, pscale, wau, wpu, wout, g, *, tm, per_row, o_transposed):
    b, t, d = x.shape
    full = lambda a: pl.BlockSpec(a.shape, lambda i, j: (0,) * a.ndim)
    rspec = lambda n: pl.BlockSpec((None, tm, n), lambda i, j: (i, j, 0))
    ospec = pl.BlockSpec((None, D_ATTN, tm), lambda i, j: (i, 0, j)) if o_transposed else rspec(D_ATTN)
    return pl.pallas_call(
        functools.partial(_post_kernel, per_row=per_row, o_transposed=o_transposed),
        grid=(b, t // tm),
        in_specs=[ospec, rspec(D_POOL), rspec(d), rspec(d), rspec(d), _mod_spec(mod, per_row, tm),
                  full(wmix), full(pscale), full(wau), full(wpu), full(wout), full(g)],
        out_specs=rspec(d),
        out_shape=jax.ShapeDtypeStruct((b, t, d), F32),
        compiler_params=_params("arbitrary", "arbitrary"),
        name="post_mix",
    )(o, dpool, ga, gb, x, mod, wmix, pscale, wau, wpu, wout, g)


def _ffn_kernel(x_ref, mod_ref, g1_ref, w1_ref, w2_ref, g2_ref, y_ref, *, per_row, fc):
    x = x_ref[...]
    if per_row:
        shift, scale, gate = mod_ref[3], mod_ref[4], mod_ref[5]
    else:
        shift, scale, gate = mod_ref[3:4, :], mod_ref[4:5, :], mod_ref[5:6, :]
    hb = (_rms(x, g1_ref[...]) * (1.0 + scale) + shift).astype(BF16)
    acc = jnp.zeros(x.shape, F32)
    for c in range(w1_ref.shape[1] // fc):
        a = jnp.dot(hb, w1_ref[:, c * fc:(c + 1) * fc], preferred_element_type=F32)
        a = jnp.square(jnp.maximum(a, 0.0))
        acc = acc + jnp.dot(a.astype(BF16), w2_ref[c * fc:(c + 1) * fc, :], preferred_element_type=F32)
    y_ref[...] = x + gate * _rms(acc, g2_ref[...])


def _ffn(x, mod, g1, w1, w2, g2, *, tm, per_row):
    b, t, d = x.shape
    full = lambda a: pl.BlockSpec(a.shape, lambda i, j: (0,) * a.ndim)
    rspec = lambda n: pl.BlockSpec((None, tm, n), lambda i, j: (i, j, 0))
    return pl.pallas_call(
        functools.partial(_ffn_kernel, per_row=per_row, fc=min(1024, w1.shape[1])),
        grid=(b, t // tm),
        in_specs=[rspec(d), _mod_spec(mod, per_row, tm), full(g1), full(w1), full(w2), full(g2)],
        out_specs=rspec(d),
        out_shape=jax.ShapeDtypeStruct((b, t, d), F32),
        compiler_params=_params("arbitrary", "arbitrary"),
        name="ffn",
    )(x, mod, g1, w1, w2, g2)


def _placement_constants(tm, page, t_s):
    ii = np.arange(tm)
    triu = (ii[:, None] <= ii[None, :]).astype(np.float32)
    jj = np.arange(page)
    mstrict = (jj[:, None] > jj[None, :]).astype(np.float32)
    pk = np.zeros((LANES, N_HEADS * SLAB), np.float32)
    pqT = np.zeros((N_HEADS * SLAB, LANES), np.float32)
    for h in range(N_HEADS):
        for p in range(N_PIECES):
            pk[p * N_HEADS + h, h * SLAB + HEAD_DIM + p] = -1.0
            pk[ONES_SLOT, h * SLAB + HEAD_DIM + N_PIECES + p] = 1.0
            pqT[h * SLAB + HEAD_DIM + p, ONES_SLOT] = 1.0
            pqT[h * SLAB + HEAD_DIM + N_PIECES + p, p * N_HEADS + h] = 1.0
    vones = np.zeros((N_HEADS * V_ROWS, 1), np.float32)
    vones[np.arange(N_HEADS) * V_ROWS + HEAD_DIM] = 1.0
    head_of_col = np.arange(D_ATTN) // HEAD_DIM
    bd_mask = head_of_col[None, :] == (np.arange(t_s * N_HEADS) % N_HEADS)[:, None]
    bf = lambda a: jnp.asarray(a, BF16)
    return dict(triu=bf(triu), tril=bf(triu.T), mstrict=bf(mstrict), pk=bf(pk), pqT=bf(pqT),
                vones=jnp.asarray(vones), bd_mask=jnp.asarray(bd_mask))


def _slab_cols(w, width):
    d = w.shape[0]
    w = w.reshape(d, N_HEADS, HEAD_DIM)
    return jnp.pad(w, ((0, 0), (0, 0), (0, width - HEAD_DIM))).reshape(d, N_HEADS * width)


def kernel(x_prompt, x_sample, cache_k, cache_v, cache_logf, state_pool, page_table, c_prompt, c_sample, w_ada, b_ada, g_pre_mix, w_in, b_forget, w_pool_mix, pool_scale, w_attn_up, w_pool_up, w_out, g_post_mix, g_pre_ffn, w_ff1, w_ff2, g_post_ffn):
    depth = w_in.shape[0]
    nb_p, t_p, d = x_prompt.shape
    nb_s, t_s, _ = x_sample.shape
    m_s = nb_s * t_s
    n_phys, page = cache_k.shape[1], cache_k.shape[2]
    n_pages = page_table.shape[1]
    assert t_p % HALO == 0 and t_p >= HALO and t_s <= POOL_BUF

    tm = min(512, t_p)
    bq = bk = min(512, t_p)
    n_pg = min(16, n_pages)
    assert t_p % tm == 0 and n_pages % n_pg == 0

    n_c = nb_p + nb_s
    pad = (-n_c) % 8
    c_all = jnp.concatenate([c_prompt, c_sample, jnp.zeros((pad, d), F32)], axis=0)
    mod_all = _modulation(c_all, w_ada, b_ada)

    ckT = cache_k.transpose(0, 1, 3, 4, 2).reshape(depth, n_phys, D_ATTN, page)
    cvT = cache_v.transpose(0, 1, 3, 4, 2).reshape(depth, n_phys, D_ATTN, page)
    clfT = cache_logf.transpose(0, 1, 3, 2)
    pt_flat = page_table.reshape(-1).astype(jnp.int32)

    cst = _placement_constants(tm, page, t_s)

    o_q, o_k, o_v, o_f = D_ATTN, 2 * D_ATTN, 3 * D_ATTN, 3 * D_ATTN + N_HEADS
    o_u = o_f + D_POOL

    xp = x_prompt
    xs = x_sample.reshape(1, m_s, d)
    prev = None
    pp, ks, vs, ls, ps = [], [], [], [], []
    for l in range(depth):
        wl = w_in[l]
        wq = (wl[:, :o_q] * QK_SCALE).astype(BF16)
        wkv = wl[:, o_q:o_v].astype(BF16)
        wf = wl[:, o_v:o_f]
        bfl = b_forget[l].reshape(1, N_HEADS)
        wf_pad = jnp.pad(wf, ((0, 0), (0, LANES - N_HEADS))).astype(BF16)
        bf_pad = jnp.pad(bfl, ((0, 0), (0, LANES - N_HEADS)))
        wf_rep = jnp.pad(jnp.tile(wf, (1, N_PIECES)), ((0, 0), (0, LANES - ONES_SLOT))).astype(BF16)
        bf_rep = jnp.pad(jnp.tile(bfl, (1, N_PIECES)), ((0, 0), (0, LANES - ONES_SLOT)))
        wfT_rep = jnp.pad(jnp.tile(wf.T, (N_PIECES, 1)), ((0, N_HEADS), (0, 0))).astype(BF16)
        bfT_rep = jnp.pad(jnp.tile(bfl.T, (N_PIECES, 1)), ((0, N_HEADS), (0, 0)))
        wu = wl[:, o_f:o_u].astype(BF16)
        wgab = wl[:, o_u:].astype(BF16)
        wqaT = _slab_cols(wl[:, :o_q] * (QK_SCALE * LOG2E), SLAB).T.astype(BF16)
        wka = _slab_cols(wl[:, o_q:o_k], SLAB).astype(BF16)
        wkT = wl[:, o_q:o_k].T.astype(BF16)
        wvaT = _slab_cols(wl[:, o_k:o_v], V_ROWS).T.astype(BF16)
        wmix = w_pool_mix[l].astype(BF16)
        pscale = pool_scale[l].reshape(1, D_POOL)
        wau = w_attn_up[l].astype(BF16)
        wpu = w_pool_up[l].astype(BF16)
        wout = w_out[l].astype(BF16)
        w1 = w_ff1[l].astype(BF16)
        w2 = w_ff2[l].astype(BF16)
        g_pm = g_pre_mix[l].reshape(1, d)
        g_qm = g_post_mix[l].reshape(1, d)
        g_pf = g_pre_ffn[l].reshape(1, d)
        g_qf = g_post_ffn[l].reshape(1, d)

        mod_p = mod_all[l, :nb_p].reshape(nb_p, 6, d)
        mod_s = jnp.repeat(mod_all[l, nb_p:n_c].reshape(nb_s, 6, d), t_s, axis=0).transpose(1, 0, 2)

        consts = (g_pm, wqaT, wkT, wka, wvaT, wf_rep, bf_rep, wfT_rep, bfT_rep, wu, wgab,
                  cst["triu"], cst["tril"], cst["pk"], cst["pqT"], cst["vones"])
        qaT, ka, vaT, kT_all, vT_all, lfT_all, dpool, ga, gb, ulast = _in_prompt(
            xp, mod_p, consts, prev, layer=l, depth=depth, tm=tm)
        prev = (kT_all, vT_all, lfT_all)
        oT = _attention_prompt(qaT, ka, vaT, bq=bq, bk=bk)
        xp = _post(oT, dpool, ga, gb, xp, mod_p, wmix, pscale, wau, wpu, wout, g_qm, tm=tm, per_row=False,
                   o_transposed=True)
        xp = _ffn(xp, mod_p, g_pf, w1, w2, g_qf, tm=tm, per_row=False)
        pp.append(ulast[:, HALO - POOL_BUF:, :])

        q_s, k_s, v_s, lf_s, u_s, ga_s, gb_s = _in_sample(xs[0], mod_s, g_pm, wq, wkv, wf_pad, bf_pad, wu, wgab)
        lf_s = lf_s[:, :N_HEADS].reshape(nb_s, t_s, N_HEADS)
        ext = jnp.concatenate([state_pool[l], u_s.reshape(nb_s, t_s, D_POOL)], axis=1)
        d_s = _pool_sample(ext, t_s)
        qbd = jnp.where(cst["bd_mask"][None], jnp.repeat(q_s.reshape(nb_s, t_s, 1, D_ATTN), N_HEADS, axis=2)
                        .reshape(nb_s, t_s * N_HEADS, D_ATTN), 0.0).astype(BF16)
        o_s = _attention_sample(pt_flat, qbd, k_s.reshape(nb_s, t_s, D_ATTN), v_s.reshape(nb_s, t_s, D_ATTN),
                                lf_s.transpose(0, 2, 1), cst["mstrict"], ckT, cvT, clfT,
                                layer=l, n_pages=n_pages, n_pg=n_pg)
        xs = _post(o_s.reshape(1, m_s, D_ATTN), d_s.reshape(1, m_s, D_POOL), ga_s[None], gb_s[None], xs, mod_s,
                   wmix, pscale, wau, wpu, wout, g_qm, tm=m_s, per_row=True, o_transposed=False)
        xs = _ffn(xs, mod_s, g_pf, w1, w2, g_qf, tm=m_s, per_row=True)
        ks.append(k_s.reshape(nb_s, t_s, N_HEADS, HEAD_DIM))
        vs.append(v_s.reshape(nb_s, t_s, N_HEADS, HEAD_DIM))
        ls.append(lf_s)
        ps.append(ext[:, t_s:, :])

    kT_all, vT_all, lfT_all = prev

    def heads_last(a):
        return a.reshape(depth, nb_p, N_HEADS, HEAD_DIM, t_p).transpose(0, 1, 4, 2, 3)

    return (xp, xs.reshape(nb_s, t_s, d),
            heads_last(kT_all), heads_last(vT_all), lfT_all.transpose(0, 1, 3, 2), jnp.stack(pp),
            jnp.stack(ks), jnp.stack(vs), jnp.stack(ls), jnp.stack(ps))
```

```python
import functools

import numpy as np
import jax
import jax.numpy as jnp
from jax import lax
from jax.experimental import pallas as pl
from jax.experimental.pallas import tpu as pltpu

F32 = jnp.float32
BF16 = jnp.bfloat16

N_HEADS = 8
HEAD_DIM = 64
D_ATTN = N_HEADS * HEAD_DIM
POOL_WINDOWS = (2, 4, 8, 16)
POOL_GROUP_DIM = 128
D_POOL = len(POOL_WINDOWS) * POOL_GROUP_DIM
POOL_BUF = max(POOL_WINDOWS) - 1
HALO = 16
RMS_EPS = 1e-6
NEG = -1e30
QK_SCALE = HEAD_DIM ** -0.5
LOG2E = 1.4426950408889634

LANES = 128
SLAB = LANES
N_PIECES = 3
EXTRA = 16
V_ROWS = HEAD_DIM + EXTRA
ONES_SLOT = N_PIECES * N_HEADS

VMEM_LIMIT = 48 * 1024 * 1024

_NT = (((1,), (1,)), ((), ()))
_TN = (((0,), (0,)), ((), ()))


def _params(*sem, flags=None):
    return pltpu.CompilerParams(dimension_semantics=sem, vmem_limit_bytes=VMEM_LIMIT, flags=flags)


def _rms(x, g):
    return x * lax.rsqrt(jnp.mean(x * x, axis=-1, keepdims=True) + RMS_EPS) * g


def _log_sigmoid(x):
    return jnp.minimum(x, 0.0) - jnp.log1p(jnp.exp(-jnp.abs(x)))


def _split3(a):
    hi = a.astype(BF16).astype(F32)
    r1 = a - hi
    mid = r1.astype(BF16).astype(F32)
    lo = (r1 - mid).astype(BF16).astype(F32)
    return hi, mid, lo


def _dot_exact01(a, m01):
    return sum(jnp.dot(p.astype(BF16), m01, preferred_element_type=F32) for p in _split3(a))


def _mod_kernel(c_ref, w_ref, b_ref, o_ref):
    c = c_ref[...]
    a = (c * jax.nn.sigmoid(c)).astype(BF16)
    o_ref[...] = jnp.dot(a, w_ref[...].astype(BF16), preferred_element_type=F32) + b_ref[...]


def _modulation(c_all, w_ada, b_ada):
    depth, d, n_mod = w_ada.shape
    r = c_all.shape[0]
    tn = n_mod // 4
    return pl.pallas_call(
        _mod_kernel,
        grid=(depth, n_mod // tn),
        in_specs=[pl.BlockSpec((r, d), lambda l, j: (0, 0)),
                  pl.BlockSpec((None, d, tn), lambda l, j: (l, 0, j)),
                  pl.BlockSpec((None, 1, tn), lambda l, j: (l, 0, j))],
        out_specs=pl.BlockSpec((None, r, tn), lambda l, j: (l, 0, j)),
        out_shape=jax.ShapeDtypeStruct((depth, r, n_mod), F32),
        compiler_params=_params("arbitrary", "arbitrary"),
        name="adaln_mod",
    )(c_all, w_ada, b_ada.reshape(depth, 1, n_mod))


def _in_prompt_kernel(x_ref, mod_ref, g_ref, wT_ref, bfT_ref, wu_ref, wgab_ref, triu_ref, pq_ref, pk_ref, *rest,
                      tm, n_prev):
    (qaT_ref, ka_ref, vaT_ref, kT_ref, vT_ref, lfT_ref, d_ref, ga_ref, gb_ref, ulast_ref,
     ccol_ref, ubuf_ref) = rest[n_prev:]
    t = pl.program_id(1)
    d_model = x_ref.shape[-1]

    @pl.when(t == 0)
    def _():
        ccol_ref[...] = jnp.zeros_like(ccol_ref)
        ubuf_ref[0:HALO, :] = jnp.zeros((HALO, D_POOL), F32)

    h = _rms(x_ref[...], g_ref[...]) * (1.0 + mod_ref[1:2, :]) + mod_ref[0:1, :]
    hb = h.astype(BF16)

    zT = lax.dot_general(wT_ref[...], hb, _NT, preferred_element_type=F32)
    qT = zT[0:D_ATTN]
    kT = zT[D_ATTN:2 * D_ATTN]
    vT = zT[2 * D_ATTN:3 * D_ATTN]
    kT_ref[...] = kT
    vT_ref[...] = vT
    lfT = _log_sigmoid(zT[3 * D_ATTN:] + bfT_ref[...])
    lfT_ref[...] = lfT[0:N_HEADS]

    cT = _dot_exact01(lfT, triu_ref[...]) + ccol_ref[...]
    ccol_ref[...] = cT[:, tm - 1:tm]
    row = lax.broadcasted_iota(jnp.int32, cT.shape, 0)
    hiT, midT, loT = _split3(cT * LOG2E)
    cpT = jnp.where(row < N_HEADS, hiT, jnp.where(row < 2 * N_HEADS, midT, jnp.where(row < ONES_SLOT, loT, 1.0)))
    cpT = jnp.concatenate([cpT, jnp.zeros((LANES - cpT.shape[0], tm), F32)], axis=0).astype(BF16)
    eqT = jnp.dot(pq_ref[...], cpT, preferred_element_type=F32)
    ekT = jnp.dot(pk_ref[...], cpT, preferred_element_type=F32)

    pad_q = jnp.zeros((SLAB - HEAD_DIM - EXTRA, tm), BF16)
    pad_k = jnp.zeros((SLAB - HEAD_DIM - EXTRA, tm), F32)
    ones_blk = (lax.broadcasted_iota(jnp.int32, (EXTRA, tm), 0) == 0).astype(BF16)
    for hd in range(N_HEADS):
        f0, e0 = hd * HEAD_DIM, hd * EXTRA
        q0, v0 = hd * SLAB, hd * V_ROWS
        qaT_ref[q0:q0 + HEAD_DIM, :] = qT[f0:f0 + HEAD_DIM].astype(BF16)
        qaT_ref[q0 + HEAD_DIM:q0 + HEAD_DIM + EXTRA, :] = eqT[e0:e0 + EXTRA].astype(BF16)
        qaT_ref[q0 + HEAD_DIM + EXTRA:q0 + SLAB, :] = pad_q
        vaT_ref[v0:v0 + HEAD_DIM, :] = vT[f0:f0 + HEAD_DIM].astype(BF16)
        vaT_ref[v0 + HEAD_DIM:v0 + V_ROWS, :] = ones_blk
        kaT = jnp.concatenate([kT[f0:f0 + HEAD_DIM], ekT[e0:e0 + EXTRA], pad_k], axis=0)
        ka_ref[:, q0:q0 + SLAB] = kaT.T.astype(BF16)

    gab = jnp.dot(hb, wgab_ref[...], preferred_element_type=F32)
    ga_ref[...] = gab[:, :d_model]
    gb_ref[...] = gab[:, d_model:]

    u = jnp.dot(hb, wu_ref[...], preferred_element_type=F32)
    ubuf_ref[HALO:HALO + tm, :] = u
    pos = t * tm + lax.broadcasted_iota(jnp.int32, (tm, 1), 0)
    for g, w in enumerate(POOL_WINDOWS):
        lo_c, hi_c = g * POOL_GROUP_DIM, (g + 1) * POOL_GROUP_DIM
        ug = u[:, lo_c:hi_c]
        s = ug
        for i in range(1, w):
            s = s + ubuf_ref[HALO - i:HALO - i + tm, lo_c:hi_c]
        inv = 1.0 / jnp.minimum(pos + 1, w).astype(F32)
        d_ref[:, lo_c:hi_c] = (s * inv - ug).astype(BF16)
    tail = u[tm - HALO:, :]
    ulast_ref[...] = tail
    ubuf_ref[0:HALO, :] = tail


def _in_prompt(x, mod, consts, prev, *, layer, depth, tm):
    b, t, d = x.shape
    nt = t // tm
    row = lambda n, dt: jax.ShapeDtypeStruct((b, t, n), dt)
    col = lambda n, dt: jax.ShapeDtypeStruct((b, n, t), dt)
    lay = lambda n: jax.ShapeDtypeStruct((depth, b, n, t), F32)
    full = lambda a: pl.BlockSpec(a.shape, lambda i, j: (0,) * a.ndim)
    rspec = lambda n: pl.BlockSpec((None, tm, n), lambda i, j: (i, j, 0))
    cspec = lambda n: pl.BlockSpec((None, n, tm), lambda i, j: (i, 0, j))
    lspec = lambda n: pl.BlockSpec((None, None, n, tm), lambda i, j: (layer, i, 0, j))
    prev = () if prev is None else tuple(prev)
    n_in = 2 + len(consts)
    return pl.pallas_call(
        functools.partial(_in_prompt_kernel, tm=tm, n_prev=len(prev)),
        grid=(b, nt),
        in_specs=[rspec(d), pl.BlockSpec((None, 6, d), lambda i, j: (i, 0, 0))] + [full(c) for c in consts]
                 + [pl.BlockSpec(memory_space=pl.ANY)] * len(prev),
        out_specs=[cspec(N_HEADS * SLAB), rspec(N_HEADS * SLAB), cspec(N_HEADS * V_ROWS),
                   lspec(D_ATTN), lspec(D_ATTN), lspec(N_HEADS), rspec(D_POOL), rspec(d), rspec(d),
                   pl.BlockSpec((None, HALO, D_POOL), lambda i, j: (i, 0, 0))],
        out_shape=[col(N_HEADS * SLAB, BF16), row(N_HEADS * SLAB, BF16), col(N_HEADS * V_ROWS, BF16),
                   lay(D_ATTN), lay(D_ATTN), lay(N_HEADS), row(D_POOL, BF16), row(d, F32), row(d, F32),
                   jax.ShapeDtypeStruct((b, HALO, D_POOL), F32)],
        scratch_shapes=[pltpu.VMEM((4 * N_HEADS, 1), F32), pltpu.VMEM((tm + HALO, D_POOL), F32)],
        input_output_aliases={n_in + i: 3 + i for i in range(len(prev))},
        compiler_params=_params("arbitrary", "arbitrary"),
        name="in_proj_prompt",
    )(x, mod, *consts, *prev)


def _in_sample_kernel(x_ref, mod_ref, g_ref, wq_ref, wkv_ref, wf_ref, bf_ref, wu_ref, wgab_ref,
                      q_ref, k_ref, v_ref, lf_ref, u_ref, ga_ref, gb_ref):
    d_model = x_ref.shape[-1]
    h = _rms(x_ref[...], g_ref[...]) * (1.0 + mod_ref[1]) + mod_ref[0]
    hb = h.astype(BF16)
    q_ref[...] = jnp.dot(hb, wq_ref[...], preferred_element_type=F32)
    kv = jnp.dot(hb, wkv_ref[...], preferred_element_type=F32)
    k_ref[...] = kv[:, :D_ATTN]
    v_ref[...] = kv[:, D_ATTN:]
    f = jnp.dot(hb, wf_ref[...], preferred_element_type=F32) + bf_ref[...]
    lf_ref[...] = _log_sigmoid(f)
    u_ref[...] = jnp.dot(hb, wu_ref[...], preferred_element_type=F32)
    gab = jnp.dot(hb, wgab_ref[...], preferred_element_type=F32)
    ga_ref[...] = gab[:, :d_model]
    gb_ref[...] = gab[:, d_model:]


def _in_sample(x, mod_rows, g, wq, wkv, wf, bf, wu, wgab):
    m, d = x.shape
    out = lambda n: jax.ShapeDtypeStruct((m, n), F32)
    return pl.pallas_call(
        _in_sample_kernel,
        out_shape=[out(D_ATTN), out(D_ATTN), out(D_ATTN), out(wf.shape[1]), out(D_POOL), out(d), out(d)],
        compiler_params=pltpu.CompilerParams(vmem_limit_bytes=VMEM_LIMIT),
        name="in_proj_sample",
    )(x, mod_rows, g, wq, wkv, wf, bf, wu, wgab)


def _attn_kernel(qi_ref, kj_ref, qaT_ref, ka_ref, vaT_ref, oT_ref, m_ref, acc_ref, *, bq, bk):
    p = pl.program_id(1)
    qi = qi_ref[p]
    kj = kj_ref[p]

    @pl.when(kj == 0)
    def _():
        m_ref[...] = jnp.full(m_ref.shape, NEG, F32)
        acc_ref[...] = jnp.zeros_like(acc_ref)

    def step(masked):
        if masked:
            key = kj * bk + lax.broadcasted_iota(jnp.int32, (bk, bq), 0)
            qry = qi * bq + lax.broadcasted_iota(jnp.int32, (bk, bq), 1)
            keep = key <= qry
        scores = lambda h: jnp.dot(ka_ref[:, h * SLAB:(h + 1) * SLAB], qaT_ref[h * SLAB:(h + 1) * SLAB, :],
                                   preferred_element_type=F32)

        def softmax(h, sT):
            if masked:
                sT = jnp.where(keep, sT, NEG)
            m_old = m_ref[h:h + 1, :]
            m_new = jnp.maximum(m_old, jnp.max(sT, axis=0, keepdims=True))
            m_ref[h:h + 1, :] = m_new
            return jnp.exp2(m_old - m_new), jnp.exp2(sT - m_new).astype(BF16)

        def accumulate(h, alpha, pT):
            pv = jnp.dot(vaT_ref[h * V_ROWS:(h + 1) * V_ROWS, :], pT, preferred_element_type=F32)
            acc_ref[h] = alpha * acc_ref[h] + pv

        s_cur = scores(0)
        pending = None
        for h in range(N_HEADS):
            s_nxt = scores(h + 1) if h + 1 < N_HEADS else None
            soft = softmax(h, s_cur)
            if pending is not None:
                accumulate(h - 1, *pending)
            pending = soft
            s_cur = s_nxt
        accumulate(N_HEADS - 1, *pending)

    crosses_diag = (kj + 1) * bk - 1 > qi * bq
    pl.when(crosses_diag)(lambda: step(True))
    pl.when(jnp.logical_not(crosses_diag))(lambda: step(False))

    @pl.when(kj == pl.cdiv((qi + 1) * bq, bk) - 1)
    def _():
        for h in range(N_HEADS):
            a = acc_ref[h]
            oT_ref[h * HEAD_DIM:(h + 1) * HEAD_DIM, :] = (a[0:HEAD_DIM] / a[HEAD_DIM:HEAD_DIM + 1]).astype(BF16)


def _attention_prompt(qaT, ka, vaT, *, bq, bk):
    b, t, _ = ka.shape
    nq = t // bq
    pairs = [(i, j) for i in range(nq) for j in range(-(-((i + 1) * bq) // bk))]
    qi_tab = jnp.asarray(np.array([p[0] for p in pairs], np.int32))
    kj_tab = jnp.asarray(np.array([p[1] for p in pairs], np.int32))
    grid_spec = pltpu.PrefetchScalarGridSpec(
        num_scalar_prefetch=2,
        grid=(b, len(pairs)),
        in_specs=[pl.BlockSpec((None, N_HEADS * SLAB, bq), lambda i, p, qt, kt: (i, 0, qt[p])),
                  pl.BlockSpec((None, bk, N_HEADS * SLAB), lambda i, p, qt, kt: (i, kt[p], 0)),
                  pl.BlockSpec((None, N_HEADS * V_ROWS, bk), lambda i, p, qt, kt: (i, 0, kt[p]))],
        out_specs=pl.BlockSpec((None, D_ATTN, bq), lambda i, p, qt, kt: (i, 0, qt[p])),
        scratch_shapes=[pltpu.VMEM((N_HEADS, bq), F32), pltpu.VMEM((N_HEADS, V_ROWS, bq), F32)],
    )
    return pl.pallas_call(
        functools.partial(_attn_kernel, bq=bq, bk=bk),
        grid_spec=grid_spec,
        out_shape=jax.ShapeDtypeStruct((b, D_ATTN, t), BF16),
        compiler_params=_params("arbitrary", "arbitrary"),
        name="fox_attn_prompt",
    )(qi_tab, kj_tab, qaT, ka, vaT)


def _sattn_kernel(pt_ref, qbd_ref, knew_ref, vnew_ref, lfn_ref, mstrict_ref, clf_ref, *rest, n_pg, n_groups, n_new):
    k_refs = rest[:n_pg]
    v_refs = rest[n_pg:2 * n_pg]
    o_ref, m_ref, l_ref, acc_ref, carry_ref = rest[2 * n_pg:]
    b = pl.program_id(0)
    g = pl.program_id(1)
    first_slot = (b * n_groups + (n_groups - 1 - g)) * n_pg
    rows = n_new * N_HEADS
    qbd = qbd_ref[...]
    tile_rows = lambda a: jnp.concatenate([a] * n_new, axis=0)

    @pl.when(g == 0)
    def _():
        qf = qbd.astype(F32)
        lfn = lfn_ref[...]
        r_idx = lax.broadcasted_iota(jnp.int32, (rows, 1), 0)
        s_cols = []
        run = jnp.zeros((N_HEADS, 1), F32)
        for tp in range(n_new):
            run = run + lfn[:, tp:tp + 1]
            s = jnp.sum(qf * knew_ref[tp:tp + 1, :], axis=1, keepdims=True) - tile_rows(run)
            s_cols.append(jnp.where(r_idx >= tp * N_HEADS, s, NEG))
        m = s_cols[0]
        for s in s_cols[1:]:
            m = jnp.maximum(m, s)
        l = jnp.zeros((rows, 1), F32)
        acc = jnp.zeros((rows, D_ATTN), F32)
        for tp, s in enumerate(s_cols):
            pm = jnp.exp(s - m)
            l = l + pm
            acc = acc + pm * vnew_ref[tp:tp + 1, :]
        m_ref[...] = m
        l_ref[...] = l
        acc_ref[...] = acc
        carry_ref[...] = jnp.zeros_like(carry_ref)

    carry = carry_ref[...]
    s_parts = []
    for i in range(n_pg):
        lf = clf_ref[pt_ref[first_slot + (n_pg - 1 - i)]]
        lf2 = jnp.concatenate([lf, lf], axis=0)
        suffix = _dot_exact01(lf2, mstrict_ref[...])[0:N_HEADS] + carry
        carry = suffix[:, 0:1] + lf[:, 0:1]
        s_parts.append(jnp.dot(qbd, k_refs[i][...].astype(BF16), preferred_element_type=F32) + tile_rows(suffix))
    carry_ref[...] = carry
    s = jnp.concatenate(s_parts, axis=1)
    m_old = m_ref[...]
    m_new = jnp.maximum(m_old, jnp.max(s, axis=1, keepdims=True))
    alpha = jnp.exp(m_old - m_new)
    pm = jnp.exp(s - m_new)
    l_ref[...] = alpha * l_ref[...] + jnp.sum(pm, axis=1, keepdims=True)
    pmb = pm.astype(BF16)
    page = s_parts[0].shape[1]
    pv = jnp.zeros(acc_ref.shape, F32)
    for i in range(n_pg):
        pv = pv + lax.dot_general(pmb[:, i * page:(i + 1) * page], v_refs[i][...].astype(BF16), _NT,
                                  preferred_element_type=F32)
    acc_ref[...] = alpha * acc_ref[...] + pv
    m_ref[...] = m_new

    @pl.when(g == n_groups - 1)
    def _():
        o = acc_ref[...] / l_ref[...]
        r_head = lax.broadcasted_iota(jnp.int32, (rows, D_ATTN), 0) % N_HEADS
        c_head = lax.broadcasted_iota(jnp.int32, (rows, D_ATTN), 1) // HEAD_DIM
        o = jnp.where(r_head == c_head, o, 0.0)
        for t in range(n_new):
            o_ref[t:t + 1, :] = jnp.sum(o[t * N_HEADS:(t + 1) * N_HEADS], axis=0, keepdims=True)


def _attention_sample(pt_flat, qbd, knew, vnew, lfnT, mstrict, ckT, cvT, clfT, *, layer, n_pages, n_pg):
    nb, rows, _ = qbd.shape
    n_new = knew.shape[1]
    page = ckT.shape[-1]
    n_groups = n_pages // n_pg

    def page_map(i):
        def f(b, g, pt):
            return (layer, pt[b * n_pages + (n_groups - 1 - g) * n_pg + (n_pg - 1 - i)], 0, 0)
        return f

    per_seq = lambda shape: pl.BlockSpec((None,) + shape, lambda b, g, pt: (b, 0, 0))
    in_specs = [per_seq((rows, D_ATTN)), per_seq((n_new, D_ATTN)), per_seq((n_new, D_ATTN)),
                per_seq((N_HEADS, n_new)), pl.BlockSpec(mstrict.shape, lambda b, g, pt: (0, 0))]
    in_specs += [pl.BlockSpec((None,) + clfT.shape[1:], lambda b, g, pt: (layer, 0, 0, 0))]
    in_specs += [pl.BlockSpec((None, None, D_ATTN, page), page_map(i)) for i in range(n_pg)]
    in_specs += [pl.BlockSpec((None, None, D_ATTN, page), page_map(i)) for i in range(n_pg)]
    grid_spec = pltpu.PrefetchScalarGridSpec(
        num_scalar_prefetch=1,
        grid=(nb, n_groups),
        in_specs=in_specs,
        out_specs=per_seq((n_new, D_ATTN)),
        scratch_shapes=[pltpu.VMEM((rows, 1), F32), pltpu.VMEM((rows, 1), F32), pltpu.VMEM((rows, D_ATTN), F32),
                        pltpu.VMEM((N_HEADS, 1), F32)],
    )
    return pl.pallas_call(
        functools.partial(_sattn_kernel, n_pg=n_pg, n_groups=n_groups, n_new=n_new),
        grid_spec=grid_spec,
        out_shape=jax.ShapeDtypeStruct((nb, n_new, D_ATTN), F32),
        compiler_params=_params("arbitrary", "arbitrary"),
        name="fox_attn_sample",
    )(pt_flat, qbd, knew, vnew, lfnT, mstrict, clfT, *([ckT] * n_pg), *([cvT] * n_pg))


def _pool_sample_kernel(ext_ref, d_ref, *, n_new):
    for g, w in enumerate(POOL_WINDOWS):
        lo, hi = g * POOL_GROUP_DIM, (g + 1) * POOL_GROUP_DIM
        u = ext_ref[:, POOL_BUF:POOL_BUF + n_new, lo:hi]
        s = u
        for i in range(1, w):
            s = s + ext_ref[:, POOL_BUF - i:POOL_BUF - i + n_new, lo:hi]
        d_ref[:, :, lo:hi] = s * (1.0 / w) - u


def _pool_sample(ext, n_new):
    nb = ext.shape[0]
    return pl.pallas_call(
        functools.partial(_pool_sample_kernel, n_new=n_new),
        out_shape=jax.ShapeDtypeStruct((nb, n_new, D_POOL), F32),
        name="pool_sample",
    )(ext)


def _post_kernel(o_ref, d_ref, ga_ref, gb_ref, x_ref, mod_ref, wmix_ref, pscale_ref, wau_ref, wpu_ref, wout_ref,
                 g_ref, y_ref, *, per_row, o_transposed):
    d = d_ref[...].astype(BF16)
    parts = [jnp.dot(d[:, g * POOL_GROUP_DIM:(g + 1) * POOL_GROUP_DIM], wmix_ref[g], preferred_element_type=F32)
             for g in range(len(POOL_WINDOWS))]
    o_pool = jnp.concatenate(parts, axis=1) * pscale_ref[...]
    if o_transposed:
        a = lax.dot_general(o_ref[...], wau_ref[...], _TN, preferred_element_type=F32)
    else:
        a = jnp.dot(o_ref[...].astype(BF16), wau_ref[...], preferred_element_type=F32)
    b = jnp.dot(o_pool.astype(BF16), wpu_ref[...], preferred_element_type=F32)
    merged = jax.nn.sigmoid(ga_ref[...]) * a + jax.nn.sigmoid(gb_ref[...]) * b
    y = jnp.dot(merged.astype(BF16), wout_ref[...], preferred_element_type=F32)
    gate = mod_ref[2] if per_row else mod_ref[2:3, :]
    y_ref[...] = x_ref[...] + gate * _rms(y, g_ref[...])


def _mod_spec(mod, per_row, tm):
    if per_row:
        return pl.BlockSpec((6, tm, mod.shape[-1]), lambda i, j: (0, j, 0))
    return pl.BlockSpec((None, 6, mod.shape[-1]), lambda i, j: (i, 0, 0))


def _post(o, dpool, ga, gb, x, mod, wmix, pscale, wau, wpu, wout, g, *, tm, per_row, o_transposed):
    b, t, d = x.shape
    full = lambda a: pl.BlockSpec(a.shape, lambda i, j: (0,) * a.ndim)
    rspec = lambda n: pl.BlockSpec((None, tm, n), lambda i, j: (i, j, 0))
    ospec = pl.BlockSpec((None, D_ATTN, tm), lambda i, j: (i, 0, j)) if o_transposed else rspec(D_ATTN)
    return pl.pallas_call(
        functools.partial(_post_kernel, per_row=per_row, o_transposed=o_transposed),
        grid=(b, t // tm),
        in_specs=[ospec, rspec(D_POOL), rspec(d), rspec(d), rspec(d), _mod_spec(mod, per_row, tm),
                  full(wmix), full(pscale), full(wau), full(wpu), full(wout), full(g)],
        out_specs=rspec(d),
        out_shape=jax.ShapeDtypeStruct((b, t, d), F32),
        compiler_params=_params("arbitrary", "arbitrary"),
        name="post_mix",
    )(o, dpool, ga, gb, x, mod, wmix, pscale, wau, wpu, wout, g)


def _ffn_kernel(x_ref, mod_ref, g1_ref, w1_ref, w2_ref, g2_ref, y_ref, *, per_row, fc):
    x = x_ref[...]
    if per_row:
        shift, scale, gate = mod_ref[3], mod_ref[4], mod_ref[5]
    else:
        shift, scale, gate = mod_ref[3:4, :], mod_ref[4:5, :], mod_ref[5:6, :]
    hb = (_rms(x, g1_ref[...]) * (1.0 + scale) + shift).astype(BF16)
    acc = jnp.zeros(x.shape, F32)
    for c in range(w1_ref.shape[1] // fc):
        a = jnp.dot(hb, w1_ref[:, c * fc:(c + 1) * fc], preferred_element_type=F32)
        a = jnp.square(jnp.maximum(a, 0.0))
        acc = acc + jnp.dot(a.astype(BF16), w2_ref[c * fc:(c + 1) * fc, :], preferred_element_type=F32)
    y_ref[...] = x + gate * _rms(acc, g2_ref[...])


def _ffn(x, mod, g1, w1, w2, g2, *, tm, per_row):
    b, t, d = x.shape
    full = lambda a: pl.BlockSpec(a.shape, lambda i, j: (0,) * a.ndim)
    rspec = lambda n: pl.BlockSpec((None, tm, n), lambda i, j: (i, j, 0))
    return pl.pallas_call(
        functools.partial(_ffn_kernel, per_row=per_row, fc=min(1024, w1.shape[1])),
        grid=(b, t // tm),
        in_specs=[rspec(d), _mod_spec(mod, per_row, tm), full(g1), full(w1), full(w2), full(g2)],
        out_specs=rspec(d),
        out_shape=jax.ShapeDtypeStruct((b, t, d), F32),
        compiler_params=_params("arbitrary", "arbitrary"),
        name="ffn",
    )(x, mod, g1, w1, w2, g2)


def _placement_constants(tm, page, t_s):
    ii = np.arange(tm)
    triu = (ii[:, None] <= ii[None, :]).astype(np.float32)
    jj = np.arange(page)
    mstrict = (jj[:, None] > jj[None, :]).astype(np.float32)
    pq = np.zeros((N_HEADS * EXTRA, LANES), np.float32)
    pk = np.zeros((N_HEADS * EXTRA, LANES), np.float32)
    for h in range(N_HEADS):
        for p in range(N_PIECES):
            pq[h * EXTRA + p, ONES_SLOT] = 1.0
            pq[h * EXTRA + N_PIECES + p, p * N_HEADS + h] = 1.0
            pk[h * EXTRA + p, p * N_HEADS + h] = -1.0
            pk[h * EXTRA + N_PIECES + p, ONES_SLOT] = 1.0
    head_of_col = np.arange(D_ATTN) // HEAD_DIM
    bd_mask = head_of_col[None, :] == (np.arange(t_s * N_HEADS) % N_HEADS)[:, None]
    bf = lambda a: jnp.asarray(a, BF16)
    return dict(triu=bf(triu), mstrict=bf(mstrict), pq=bf(pq), pk=bf(pk), bd_mask=jnp.asarray(bd_mask))


def kernel(x_prompt, x_sample, cache_k, cache_v, cache_logf, state_pool, page_table, c_prompt, c_sample, w_ada, b_ada, g_pre_mix, w_in, b_forget, w_pool_mix, pool_scale, w_attn_up, w_pool_up, w_out, g_post_mix, g_pre_ffn, w_ff1, w_ff2, g_post_ffn):
    depth = w_in.shape[0]
    nb_p, t_p, d = x_prompt.shape
    nb_s, t_s, _ = x_sample.shape
    m_s = nb_s * t_s
    n_phys, page = cache_k.shape[1], cache_k.shape[2]
    n_pages = page_table.shape[1]
    assert t_p % HALO == 0 and t_p >= HALO and t_s <= POOL_BUF

    tm = min(512, t_p)
    bq, bk = min(512, t_p), min(1024, t_p)
    n_pg = min(16, n_pages)
    assert t_p % tm == 0 and t_p % bq == 0 and t_p % bk == 0 and n_pages % n_pg == 0

    n_c = nb_p + nb_s
    pad = (-n_c) % 8
    c_all = jnp.concatenate([c_prompt, c_sample, jnp.zeros((pad, d), F32)], axis=0)
    mod_all = _modulation(c_all, w_ada, b_ada)

    ckT = cache_k.transpose(0, 1, 3, 4, 2).reshape(depth, n_phys, D_ATTN, page)
    cvT = cache_v.transpose(0, 1, 3, 4, 2).reshape(depth, n_phys, D_ATTN, page)
    clfT = cache_logf.transpose(0, 1, 3, 2)
    pt_flat = page_table.reshape(-1).astype(jnp.int32)

    cst = _placement_constants(tm, page, t_s)

    o_q, o_k, o_v, o_f = D_ATTN, 2 * D_ATTN, 3 * D_ATTN, 3 * D_ATTN + N_HEADS
    o_u = o_f + D_POOL

    xp = x_prompt
    xs = x_sample.reshape(1, m_s, d)
    prev = None
    pp, ks, vs, ls, ps = [], [], [], [], []
    for l in range(depth):
        wl = w_in[l]
        wq = (wl[:, :o_q] * QK_SCALE).astype(BF16)
        wkv = wl[:, o_q:o_v].astype(BF16)
        wf = wl[:, o_v:o_f]
        bfl = b_forget[l].reshape(1, N_HEADS)
        wf_pad = jnp.pad(wf, ((0, 0), (0, LANES - N_HEADS))).astype(BF16)
        bf_pad = jnp.pad(bfl, ((0, 0), (0, LANES - N_HEADS)))
        wfT_rep = jnp.pad(jnp.tile(wf.T, (N_PIECES, 1)), ((0, N_HEADS), (0, 0)))
        bfT_rep = jnp.pad(jnp.tile(bfl.T, (N_PIECES, 1)), ((0, N_HEADS), (0, 0)))
        wT = jnp.concatenate([(wl[:, :o_q] * (QK_SCALE * LOG2E)).T, wl[:, o_q:o_v].T, wfT_rep], axis=0).astype(BF16)
        wu = wl[:, o_f:o_u].astype(BF16)
        wgab = wl[:, o_u:].astype(BF16)
        wmix = w_pool_mix[l].astype(BF16)
        pscale = pool_scale[l].reshape(1, D_POOL)
        wau = w_attn_up[l].astype(BF16)
        wpu = w_pool_up[l].astype(BF16)
        wout = w_out[l].astype(BF16)
        w1 = w_ff1[l].astype(BF16)
        w2 = w_ff2[l].astype(BF16)
        g_pm = g_pre_mix[l].reshape(1, d)
        g_qm = g_post_mix[l].reshape(1, d)
        g_pf = g_pre_ffn[l].reshape(1, d)
        g_qf = g_post_ffn[l].reshape(1, d)

        mod_p = mod_all[l, :nb_p].reshape(nb_p, 6, d)
        mod_s = jnp.repeat(mod_all[l, nb_p:n_c].reshape(nb_s, 6, d), t_s, axis=0).transpose(1, 0, 2)

        consts = (g_pm, wT, bfT_rep, wu, wgab, cst["triu"], cst["pq"], cst["pk"])
        qaT, ka, vaT, kT_all, vT_all, lfT_all, dpool, ga, gb, ulast = _in_prompt(
            xp, mod_p, consts, prev, layer=l, depth=depth, tm=tm)
        prev = (kT_all, vT_all, lfT_all)
        oT = _attention_prompt(qaT, ka, vaT, bq=bq, bk=bk)
        xp = _post(oT, dpool, ga, gb, xp, mod_p, wmix, pscale, wau, wpu, wout, g_qm, tm=tm, per_row=False,
                   o_transposed=True)
        xp = _ffn(xp, mod_p, g_pf, w1, w2, g_qf, tm=tm, per_row=False)
        pp.append(ulast[:, HALO - POOL_BUF:, :])

        q_s, k_s, v_s, lf_s, u_s, ga_s, gb_s = _in_sample(xs[0], mod_s, g_pm, wq, wkv, wf_pad, bf_pad, wu, wgab)
        lf_s = lf_s[:, :N_HEADS].reshape(nb_s, t_s, N_HEADS)
        ext = jnp.concatenate([state_pool[l], u_s.reshape(nb_s, t_s, D_POOL)], axis=1)
        d_s = _pool_sample(ext, t_s)
        qbd = jnp.where(cst["bd_mask"][None], jnp.repeat(q_s.reshape(nb_s, t_s, 1, D_ATTN), N_HEADS, axis=2)
                        .reshape(nb_s, t_s * N_HEADS, D_ATTN), 0.0).astype(BF16)
        o_s = _attention_sample(pt_flat, qbd, k_s.reshape(nb_s, t_s, D_ATTN), v_s.reshape(nb_s, t_s, D_ATTN),
                                lf_s.transpose(0, 2, 1), cst["mstrict"], ckT, cvT, clfT,
                                layer=l, n_pages=n_pages, n_pg=n_pg)
        xs = _post(o_s.reshape(1, m_s, D_ATTN), d_s.reshape(1, m_s, D_POOL), ga_s[None], gb_s[None], xs, mod_s,
                   wmix, pscale, wau, wpu, wout, g_qm, tm=m_s, per_row=True, o_transposed=False)
        xs = _ffn(xs, mod_s, g_pf, w1, w2, g_qf, tm=m_s, per_row=True)
        ks.append(k_s.reshape(nb_s, t_s, N_HEADS, HEAD_DIM))
        vs.append(v_s.reshape(nb_s, t_s, N_HEADS, HEAD_DIM))
        ls.append(lf_s)
        ps.append(ext[:, t_s:, :])

    kT_all, vT_all, lfT_all = prev

    def heads_last(a):
        return a.reshape(depth, nb_p, N_HEADS, HEAD_DIM, t_p).transpose(0, 1, 4, 2, 3)

    return (xp, xs.reshape(nb_s, t_s, d),
            heads_last(kT_all), heads_last(vT_all), lfT_all.transpose(0, 1, 3, 2), jnp.stack(pp),
            jnp.stack(ks), jnp.stack(vs), jnp.stack(ls), jnp.stack(ps))
```

```python
import functools

import numpy as np
import jax
import jax.numpy as jnp
from jax import lax
from jax.experimental import pallas as pl
from jax.experimental.pallas import tpu as pltpu

F32 = jnp.float32
BF16 = jnp.bfloat16

N_HEADS = 8
HEAD_DIM = 64
D_ATTN = N_HEADS * HEAD_DIM
POOL_WINDOWS = (2, 4, 8, 16)
POOL_GROUP_DIM = 128
D_POOL = len(POOL_WINDOWS) * POOL_GROUP_DIM
POOL_BUF = max(POOL_WINDOWS) - 1
HALO = 16
RMS_EPS = 1e-6
NEG = -1e30
QK_SCALE = HEAD_DIM ** -0.5
LOG2E = 1.4426950408889634

LANES = 128
SLAB = LANES
N_PIECES = 3
EXTRA = 16
V_ROWS = HEAD_DIM + EXTRA
ONES_SLOT = N_PIECES * N_HEADS

VMEM_LIMIT = 48 * 1024 * 1024

_NT = (((1,), (1,)), ((), ()))
_TN = (((0,), (0,)), ((), ()))


def _params(*sem, flags=None):
    return pltpu.CompilerParams(dimension_semantics=sem, vmem_limit_bytes=VMEM_LIMIT, flags=flags)


def _rms(x, g):
    return x * lax.rsqrt(jnp.mean(x * x, axis=-1, keepdims=True) + RMS_EPS) * g


def _log_sigmoid(x):
    return jnp.minimum(x, 0.0) - jnp.log1p(jnp.exp(-jnp.abs(x)))


def _split3(a):
    hi = a.astype(BF16).astype(F32)
    r1 = a - hi
    mid = r1.astype(BF16).astype(F32)
    lo = (r1 - mid).astype(BF16).astype(F32)
    return hi, mid, lo


def _dot_exact01(a, m01):
    return sum(jnp.dot(p.astype(BF16), m01, preferred_element_type=F32) for p in _split3(a))


def _mod_kernel(c_ref, w_ref, b_ref, o_ref):
    c = c_ref[...]
    a = (c * jax.nn.sigmoid(c)).astype(BF16)
    o_ref[...] = jnp.dot(a, w_ref[...].astype(BF16), preferred_element_type=F32) + b_ref[...]


def _modulation(c_all, w_ada, b_ada):
    depth, d, n_mod = w_ada.shape
    r = c_all.shape[0]
    tn = n_mod // 4
    return pl.pallas_call(
        _mod_kernel,
        grid=(depth, n_mod // tn),
        in_specs=[pl.BlockSpec((r, d), lambda l, j: (0, 0)),
                  pl.BlockSpec((None, d, tn), lambda l, j: (l, 0, j)),
                  pl.BlockSpec((None, 1, tn), lambda l, j: (l, 0, j))],
        out_specs=pl.BlockSpec((None, r, tn), lambda l, j: (l, 0, j)),
        out_shape=jax.ShapeDtypeStruct((depth, r, n_mod), F32),
        compiler_params=_params("arbitrary", "arbitrary"),
        name="adaln_mod",
    )(c_all, w_ada, b_ada.reshape(depth, 1, n_mod))


def _in_prompt_kernel(x_ref, mod_ref, g_ref, wT_ref, bfT_ref, wu_ref, wgab_ref, triu_ref, pq_ref, pk_ref, *rest,
                      tm, n_prev):
    (qaT_ref, ka_ref, vaT_ref, kT_ref, vT_ref, lfT_ref, d_ref, ga_ref, gb_ref, ulast_ref,
     ccol_ref, ubuf_ref) = rest[n_prev:]
    t = pl.program_id(1)
    d_model = x_ref.shape[-1]

    @pl.when(t == 0)
    def _():
        ccol_ref[...] = jnp.zeros_like(ccol_ref)
        ubuf_ref[0:HALO, :] = jnp.zeros((HALO, D_POOL), F32)

    h = _rms(x_ref[...], g_ref[...]) * (1.0 + mod_ref[1:2, :]) + mod_ref[0:1, :]
    hb = h.astype(BF16)

    zT = lax.dot_general(wT_ref[...], hb, _NT, preferred_element_type=F32)
    qT = zT[0:D_ATTN]
    kT = zT[D_ATTN:2 * D_ATTN]
    vT = zT[2 * D_ATTN:3 * D_ATTN]
    kT_ref[...] = kT
    vT_ref[...] = vT
    lfT = _log_sigmoid(zT[3 * D_ATTN:] + bfT_ref[...])
    lfT_ref[...] = lfT[0:N_HEADS]

    cT = _dot_exact01(lfT, triu_ref[...]) + ccol_ref[...]
    ccol_ref[...] = cT[:, tm - 1:tm]
    row = lax.broadcasted_iota(jnp.int32, cT.shape, 0)
    hiT, midT, loT = _split3(cT * LOG2E)
    cpT = jnp.where(row < N_HEADS, hiT, jnp.where(row < 2 * N_HEADS, midT, jnp.where(row < ONES_SLOT, loT, 1.0)))
    cpT = jnp.concatenate([cpT, jnp.zeros((LANES - cpT.shape[0], tm), F32)], axis=0).astype(BF16)
    eqT = jnp.dot(pq_ref[...], cpT, preferred_element_type=F32)
    ekT = jnp.dot(pk_ref[...], cpT, preferred_element_type=F32)

    pad_q = jnp.zeros((SLAB - HEAD_DIM - EXTRA, tm), BF16)
    pad_k = jnp.zeros((SLAB - HEAD_DIM - EXTRA, tm), F32)
    ones_blk = (lax.broadcasted_iota(jnp.int32, (EXTRA, tm), 0) == 0).astype(BF16)
    for hd in range(N_HEADS):
        f0, e0 = hd * HEAD_DIM, hd * EXTRA
        q0, v0 = hd * SLAB, hd * V_ROWS
        qaT_ref[q0:q0 + HEAD_DIM, :] = qT[f0:f0 + HEAD_DIM].astype(BF16)
        qaT_ref[q0 + HEAD_DIM:q0 + HEAD_DIM + EXTRA, :] = eqT[e0:e0 + EXTRA].astype(BF16)
        qaT_ref[q0 + HEAD_DIM + EXTRA:q0 + SLAB, :] = pad_q
        vaT_ref[v0:v0 + HEAD_DIM, :] = vT[f0:f0 + HEAD_DIM].astype(BF16)
        vaT_ref[v0 + HEAD_DIM:v0 + V_ROWS, :] = ones_blk
        kaT = jnp.concatenate([kT[f0:f0 + HEAD_DIM], ekT[e0:e0 + EXTRA], pad_k], axis=0)
        ka_ref[:, q0:q0 + SLAB] = kaT.T.astype(BF16)

    gab = jnp.dot(hb, wgab_ref[...], preferred_element_type=F32)
    ga_ref[...] = gab[:, :d_model]
    gb_ref[...] = gab[:, d_model:]

    u = jnp.dot(hb, wu_ref[...], preferred_element_type=F32)
    ubuf_ref[HALO:HALO + tm, :] = u
    pos = t * tm + lax.broadcasted_iota(jnp.int32, (tm, 1), 0)
    for g, w in enumerate(POOL_WINDOWS):
        lo_c, hi_c = g * POOL_GROUP_DIM, (g + 1) * POOL_GROUP_DIM
        ug = u[:, lo_c:hi_c]
        s = ug
        for i in range(1, w):
            s = s + ubuf_ref[HALO - i:HALO - i + tm, lo_c:hi_c]
        inv = 1.0 / jnp.minimum(pos + 1, w).astype(F32)
        d_ref[:, lo_c:hi_c] = (s * inv - ug).astype(BF16)
    tail = u[tm - HALO:, :]
    ulast_ref[...] = tail
    ubuf_ref[0:HALO, :] = tail


def _in_prompt(x, mod, consts, prev, *, layer, depth, tm):
    b, t, d = x.shape
    nt = t // tm
    row = lambda n, dt: jax.ShapeDtypeStruct((b, t, n), dt)
    col = lambda n, dt: jax.ShapeDtypeStruct((b, n, t), dt)
    lay = lambda n: jax.ShapeDtypeStruct((depth, b, n, t), F32)
    full = lambda a: pl.BlockSpec(a.shape, lambda i, j: (0,) * a.ndim)
    rspec = lambda n: pl.BlockSpec((None, tm, n), lambda i, j: (i, j, 0))
    cspec = lambda n: pl.BlockSpec((None, n, tm), lambda i, j: (i, 0, j))
    lspec = lambda n: pl.BlockSpec((None, None, n, tm), lambda i, j: (layer, i, 0, j))
    prev = () if prev is None else tuple(prev)
    n_in = 2 + len(consts)
    return pl.pallas_call(
        functools.partial(_in_prompt_kernel, tm=tm, n_prev=len(prev)),
        grid=(b, nt),
        in_specs=[rspec(d), pl.BlockSpec((None, 6, d), lambda i, j: (i, 0, 0))] + [full(c) for c in consts]
                 + [pl.BlockSpec(memory_space=pl.ANY)] * len(prev),
        out_specs=[cspec(N_HEADS * SLAB), rspec(N_HEADS * SLAB), cspec(N_HEADS * V_ROWS),
                   lspec(D_ATTN), lspec(D_ATTN), lspec(N_HEADS), rspec(D_POOL), rspec(d), rspec(d),
                   pl.BlockSpec((None, HALO, D_POOL), lambda i, j: (i, 0, 0))],
        out_shape=[col(N_HEADS * SLAB, BF16), row(N_HEADS * SLAB, BF16), col(N_HEADS * V_ROWS, BF16),
                   lay(D_ATTN), lay(D_ATTN), lay(N_HEADS), row(D_POOL, BF16), row(d, F32), row(d, F32),
                   jax.ShapeDtypeStruct((b, HALO, D_POOL), F32)],
        scratch_shapes=[pltpu.VMEM((4 * N_HEADS, 1), F32), pltpu.VMEM((tm + HALO, D_POOL), F32)],
        input_output_aliases={n_in + i: 3 + i for i in range(len(prev))},
        compiler_params=_params("arbitrary", "arbitrary"),
        name="in_proj_prompt",
    )(x, mod, *consts, *prev)


def _in_sample_kernel(x_ref, mod_ref, g_ref, wq_ref, wkv_ref, wf_ref, bf_ref, wu_ref, wgab_ref,
                      q_ref, k_ref, v_ref, lf_ref, u_ref, ga_ref, gb_ref):
    d_model = x_ref.shape[-1]
    h = _rms(x_ref[...], g_ref[...]) * (1.0 + mod_ref[1]) + mod_ref[0]
    hb = h.astype(BF16)
    q_ref[...] = jnp.dot(hb, wq_ref[...], preferred_element_type=F32)
    kv = jnp.dot(hb, wkv_ref[...], preferred_element_type=F32)
    k_ref[...] = kv[:, :D_ATTN]
    v_ref[...] = kv[:, D_ATTN:]
    f = jnp.dot(hb, wf_ref[...], preferred_element_type=F32) + bf_ref[...]
    lf_ref[...] = _log_sigmoid(f)
    u_ref[...] = jnp.dot(hb, wu_ref[...], preferred_element_type=F32)
    gab = jnp.dot(hb, wgab_ref[...], preferred_element_type=F32)
    ga_ref[...] = gab[:, :d_model]
    gb_ref[...] = gab[:, d_model:]


def _in_sample(x, mod_rows, g, wq, wkv, wf, bf, wu, wgab):
    m, d = x.shape
    out = lambda n: jax.ShapeDtypeStruct((m, n), F32)
    return pl.pallas_call(
        _in_sample_kernel,
        out_shape=[out(D_ATTN), out(D_ATTN), out(D_ATTN), out(wf.shape[1]), out(D_POOL), out(d), out(d)],
        compiler_params=pltpu.CompilerParams(vmem_limit_bytes=VMEM_LIMIT),
        name="in_proj_sample",
    )(x, mod_rows, g, wq, wkv, wf, bf, wu, wgab)


def _attn_kernel(qi_ref, kj_ref, qaT_ref, ka_ref, vaT_ref, oT_ref, m_ref, acc_ref, *, bq, bk):
    p = pl.program_id(1)
    qi = qi_ref[p]
    kj = kj_ref[p]

    @pl.when(kj == 0)
    def _():
        m_ref[...] = jnp.full(m_ref.shape, NEG, F32)
        acc_ref[...] = jnp.zeros_like(acc_ref)

    def step(n_sub, diag):
        nk = n_sub * bq
        if diag:
            keep = (lax.broadcasted_iota(jnp.int32, (bq, bq), 0) <= lax.broadcasted_iota(jnp.int32, (bq, bq), 1))
        scores = lambda h: jnp.dot(ka_ref[0:nk, h * SLAB:(h + 1) * SLAB], qaT_ref[h * SLAB:(h + 1) * SLAB, :],
                                   preferred_element_type=F32)

        def softmax(h, sT):
            if diag:
                tail = jnp.where(keep, sT[nk - bq:], NEG)
                sT = tail if n_sub == 1 else jnp.concatenate([sT[:nk - bq], tail], axis=0)
            m_old = m_ref[h:h + 1, :]
            m_new = jnp.maximum(m_old, jnp.max(sT, axis=0, keepdims=True))
            m_ref[h:h + 1, :] = m_new
            return jnp.exp2(m_old - m_new), jnp.exp2(sT - m_new).astype(BF16)

        def accumulate(h, alpha, pT):
            pv = jnp.dot(vaT_ref[h * V_ROWS:(h + 1) * V_ROWS, 0:nk], pT, preferred_element_type=F32)
            acc_ref[h] = alpha * acc_ref[h] + pv

        s_cur = scores(0)
        pending = None
        for h in range(N_HEADS):
            s_nxt = scores(h + 1) if h + 1 < N_HEADS else None
            soft = softmax(h, s_cur)
            if pending is not None:
                accumulate(h - 1, *pending)
            pending = soft
            s_cur = s_nxt
        accumulate(N_HEADS - 1, *pending)

    r = bk // bq
    crosses_diag = (kj + 1) * r > qi
    pl.when(jnp.logical_not(crosses_diag))(lambda: step(r, False))
    for c in range(r):
        pl.when(jnp.logical_and(crosses_diag, qi - kj * r == c))(functools.partial(step, c + 1, True))

    @pl.when(kj == pl.cdiv((qi + 1) * bq, bk) - 1)
    def _():
        for h in range(N_HEADS):
            a = acc_ref[h]
            oT_ref[h * HEAD_DIM:(h + 1) * HEAD_DIM, :] = (a[0:HEAD_DIM] / a[HEAD_DIM:HEAD_DIM + 1]).astype(BF16)


def _attention_prompt(qaT, ka, vaT, *, bq, bk):
    b, t, _ = ka.shape
    nq = t // bq
    pairs = [(i, j) for i in range(nq) for j in range(-(-((i + 1) * bq) // bk))]
    qi_tab = jnp.asarray(np.array([p[0] for p in pairs], np.int32))
    kj_tab = jnp.asarray(np.array([p[1] for p in pairs], np.int32))
    grid_spec = pltpu.PrefetchScalarGridSpec(
        num_scalar_prefetch=2,
        grid=(b, len(pairs)),
        in_specs=[pl.BlockSpec((None, N_HEADS * SLAB, bq), lambda i, p, qt, kt: (i, 0, qt[p])),
                  pl.BlockSpec((None, bk, N_HEADS * SLAB), lambda i, p, qt, kt: (i, kt[p], 0)),
                  pl.BlockSpec((None, N_HEADS * V_ROWS, bk), lambda i, p, qt, kt: (i, 0, kt[p]))],
        out_specs=pl.BlockSpec((None, D_ATTN, bq), lambda i, p, qt, kt: (i, 0, qt[p])),
        scratch_shapes=[pltpu.VMEM((N_HEADS, bq), F32), pltpu.VMEM((N_HEADS, V_ROWS, bq), F32)],
    )
    return pl.pallas_call(
        functools.partial(_attn_kernel, bq=bq, bk=bk),
        grid_spec=grid_spec,
        out_shape=jax.ShapeDtypeStruct((b, D_ATTN, t), BF16),
        compiler_params=_params("arbitrary", "arbitrary"),
        name="fox_attn_prompt",
    )(qi_tab, kj_tab, qaT, ka, vaT)


def _sattn_kernel(pt_ref, qbd_ref, knew_ref, vnew_ref, lfn_ref, mstrict_ref, clf_ref, *rest, n_pg, n_groups, n_new):
    k_refs = rest[:n_pg]
    v_refs = rest[n_pg:2 * n_pg]
    o_ref, m_ref, l_ref, acc_ref, carry_ref = rest[2 * n_pg:]
    b = pl.program_id(0)
    g = pl.program_id(1)
    first_slot = (b * n_groups + (n_groups - 1 - g)) * n_pg
    rows = n_new * N_HEADS
    qbd = qbd_ref[...]
    tile_rows = lambda a: jnp.concatenate([a] * n_new, axis=0)

    @pl.when(g == 0)
    def _():
        qf = qbd.astype(F32)
        lfn = lfn_ref[...]
        r_idx = lax.broadcasted_iota(jnp.int32, (rows, 1), 0)
        s_cols = []
        run = jnp.zeros((N_HEADS, 1), F32)
        for tp in range(n_new):
            run = run + lfn[:, tp:tp + 1]
            s = jnp.sum(qf * knew_ref[tp:tp + 1, :], axis=1, keepdims=True) - tile_rows(run)
            s_cols.append(jnp.where(r_idx >= tp * N_HEADS, s, NEG))
        m = s_cols[0]
        for s in s_cols[1:]:
            m = jnp.maximum(m, s)
        l = jnp.zeros((rows, 1), F32)
        acc = jnp.zeros((rows, D_ATTN), F32)
        for tp, s in enumerate(s_cols):
            pm = jnp.exp(s - m)
            l = l + pm
            acc = acc + pm * vnew_ref[tp:tp + 1, :]
        m_ref[...] = m
        l_ref[...] = l
        acc_ref[...] = acc
        carry_ref[...] = jnp.zeros_like(carry_ref)

    carry = carry_ref[...]
    s_parts = []
    for i in range(n_pg):
        lf = clf_ref[pt_ref[first_slot + (n_pg - 1 - i)]]
        lf2 = jnp.concatenate([lf, lf], axis=0)
        suffix = _dot_exact01(lf2, mstrict_ref[...])[0:N_HEADS] + carry
        carry = suffix[:, 0:1] + lf[:, 0:1]
        s_parts.append(jnp.dot(qbd, k_refs[i][...].astype(BF16), preferred_element_type=F32) + tile_rows(suffix))
    carry_ref[...] = carry
    s = jnp.concatenate(s_parts, axis=1)
    m_old = m_ref[...]
    m_new = jnp.maximum(m_old, jnp.max(s, axis=1, keepdims=True))
    alpha = jnp.exp(m_old - m_new)
    pm = jnp.exp(s - m_new)
    l_ref[...] = alpha * l_ref[...] + jnp.sum(pm, axis=1, keepdims=True)
    pmb = pm.astype(BF16)
    page = s_parts[0].shape[1]
    pv = jnp.zeros(acc_ref.shape, F32)
    for i in range(n_pg):
        pv = pv + lax.dot_general(pmb[:, i * page:(i + 1) * page], v_refs[i][...].astype(BF16), _NT,
                                  preferred_element_type=F32)
    acc_ref[...] = alpha * acc_ref[...] + pv
    m_ref[...] = m_new

    @pl.when(g == n_groups - 1)
    def _():
        o = acc_ref[...] / l_ref[...]
        r_head = lax.broadcasted_iota(jnp.int32, (rows, D_ATTN), 0) % N_HEADS
        c_head = lax.broadcasted_iota(jnp.int32, (rows, D_ATTN), 1) // HEAD_DIM
        o = jnp.where(r_head == c_head, o, 0.0)
        for t in range(n_new):
            o_ref[t:t + 1, :] = jnp.sum(o[t * N_HEADS:(t + 1) * N_HEADS], axis=0, keepdims=True)


def _attention_sample(pt_flat, qbd, knew, vnew, lfnT, mstrict, ckT, cvT, clfT, *, layer, n_pages, n_pg):
    nb, rows, _ = qbd.shape
    n_new = knew.shape[1]
    page = ckT.shape[-1]
    n_groups = n_pages // n_pg

    def page_map(i):
        def f(b, g, pt):
            return (layer, pt[b * n_pages + (n_groups - 1 - g) * n_pg + (n_pg - 1 - i)], 0, 0)
        return f

    per_seq = lambda shape: pl.BlockSpec((None,) + shape, lambda b, g, pt: (b, 0, 0))
    in_specs = [per_seq((rows, D_ATTN)), per_seq((n_new, D_ATTN)), per_seq((n_new, D_ATTN)),
                per_seq((N_HEADS, n_new)), pl.BlockSpec(mstrict.shape, lambda b, g, pt: (0, 0))]
    in_specs += [pl.BlockSpec((None,) + clfT.shape[1:], lambda b, g, pt: (layer, 0, 0, 0))]
    in_specs += [pl.BlockSpec((None, None, D_ATTN, page), page_map(i)) for i in range(n_pg)]
    in_specs += [pl.BlockSpec((None, None, D_ATTN, page), page_map(i)) for i in range(n_pg)]
    grid_spec = pltpu.PrefetchScalarGridSpec(
        num_scalar_prefetch=1,
        grid=(nb, n_groups),
        in_specs=in_specs,
        out_specs=per_seq((n_new, D_ATTN)),
        scratch_shapes=[pltpu.VMEM((rows, 1), F32), pltpu.VMEM((rows, 1), F32), pltpu.VMEM((rows, D_ATTN), F32),
                        pltpu.VMEM((N_HEADS, 1), F32)],
    )
    return pl.pallas_call(
        functools.partial(_sattn_kernel, n_pg=n_pg, n_groups=n_groups, n_new=n_new),
        grid_spec=grid_spec,
        out_shape=jax.ShapeDtypeStruct((nb, n_new, D_ATTN), F32),
        compiler_params=_params("arbitrary", "arbitrary"),
        name="fox_attn_sample",
    )(pt_flat, qbd, knew, vnew, lfnT, mstrict, clfT, *([ckT] * n_pg), *([cvT] * n_pg))


def _pool_sample_kernel(ext_ref, d_ref, *, n_new):
    for g, w in enumerate(POOL_WINDOWS):
        lo, hi = g * POOL_GROUP_DIM, (g + 1) * POOL_GROUP_DIM
        u = ext_ref[:, POOL_BUF:POOL_BUF + n_new, lo:hi]
        s = u
        for i in range(1, w):
            s = s + ext_ref[:, POOL_BUF - i:POOL_BUF - i + n_new, lo:hi]
        d_ref[:, :, lo:hi] = s * (1.0 / w) - u


def _pool_sample(ext, n_new):
    nb = ext.shape[0]
    return pl.pallas_call(
        functools.partial(_pool_sample_kernel, n_new=n_new),
        out_shape=jax.ShapeDtypeStruct((nb, n_new, D_POOL), F32),
        name="pool_sample",
    )(ext)


def _post_kernel(o_ref, d_ref, ga_ref, gb_ref, x_ref, mod_ref, wmix_ref, pscale_ref, wau_ref, wpu_ref, wout_ref,
                 g_ref, y_ref, *, per_row, o_transposed):
    d = d_ref[...].astype(BF16)
    parts = [jnp.dot(d[:, g * POOL_GROUP_DIM:(g + 1) * POOL_GROUP_DIM], wmix_ref[g], preferred_element_type=F32)
             for g in range(len(POOL_WINDOWS))]
    o_pool = jnp.concatenate(parts, axis=1) * pscale_ref[...]
    if o_transposed:
        a = lax.dot_general(o_ref[...], wau_ref[...], _TN, preferred_element_type=F32)
    else:
        a = jnp.dot(o_ref[...].astype(BF16), wau_ref[...], preferred_element_type=F32)
    b = jnp.dot(o_pool.astype(BF16), wpu_ref[...], preferred_element_type=F32)
    merged = jax.nn.sigmoid(ga_ref[...]) * a + jax.nn.sigmoid(gb_ref[...]) * b
    y = jnp.dot(merged.astype(BF16), wout_ref[...], preferred_element_type=F32)
    gate = mod_ref[2] if per_row else mod_ref[2:3, :]
    y_ref[...] = x_ref[...] + gate * _rms(y, g_ref[...])


def _mod_spec(mod, per_row, tm):
    if per_row:
        return pl.BlockSpec((6, tm, mod.shape[-1]), lambda i, j: (0, j, 0))
    return pl.BlockSpec((None, 6, mod.shape[-1]), lambda i, j: (i, 0, 0))


def _post(o, dpool, ga, gb, x, mod, wmix, pscale, wau, wpu, wout, g, *, tm, per_row, o_transposed):
    b, t, d = x.shape
    full = lambda a: pl.BlockSpec(a.shape, lambda i, j: (0,) * a.ndim)
    rspec = lambda n: pl.BlockSpec((None, tm, n), lambda i, j: (i, j, 0))
    ospec = pl.BlockSpec((None, D_ATTN, tm), lambda i, j: (i, 0, j)) if o_transposed else rspec(D_ATTN)
    return pl.pallas_call(
        functools.partial(_post_kernel, per_row=per_row, o_transposed=o_transposed),
        grid=(b, t // tm),
        in_specs=[ospec, rspec(D_POOL), rspec(d), rspec(d), rspec(d), _mod_spec(mod, per_row, tm),
                  full(wmix), full(pscale), full(wau), full(wpu), full(wout), full(g)],
        out_specs=rspec(d),
        out_shape=jax.ShapeDtypeStruct((b, t, d), F32),
        compiler_params=_params("arbitrary", "arbitrary"),
        name="post_mix",
    )(o, dpool, ga, gb, x, mod, wmix, pscale, wau, wpu, wout, g)


def _ffn_kernel(x_ref, mod_ref, g1_ref, w1_ref, w2_ref, g2_ref, y_ref, *, per_row, fc):
    x = x_ref[...]
    if per_row:
        shift, scale, gate = mod_ref[3], mod_ref[4], mod_ref[5]
    else:
        shift, scale, gate = mod_ref[3:4, :], mod_ref[4:5, :], mod_ref[5:6, :]
    hb = (_rms(x, g1_ref[...]) * (1.0 + scale) + shift).astype(BF16)
    acc = jnp.zeros(x.shape, F32)
    for c in range(w1_ref.shape[1] // fc):
        a = jnp.dot(hb, w1_ref[:, c * fc:(c + 1) * fc], preferred_element_type=F32)
        a = jnp.square(jnp.maximum(a, 0.0))
        acc = acc + jnp.dot(a.astype(BF16), w2_ref[c * fc:(c + 1) * fc, :], preferred_element_type=F32)
    y_ref[...] = x + gate * _rms(acc, g2_ref[...])


def _ffn(x, mod, g1, w1, w2, g2, *, tm, per_row):
    b, t, d = x.shape
    full = lambda a: pl.BlockSpec(a.shape, lambda i, j: (0,) * a.ndim)
    rspec = lambda n: pl.BlockSpec((None, tm, n), lambda i, j: (i, j, 0))
    return pl.pallas_call(
        functools.partial(_ffn_kernel, per_row=per_row, fc=min(1024, w1.shape[1])),
        grid=(b, t // tm),
        in_specs=[rspec(d), _mod_spec(mod, per_row, tm), full(g1), full(w1), full(w2), full(g2)],
        out_specs=rspec(d),
        out_shape=jax.ShapeDtypeStruct((b, t, d), F32),
        compiler_params=_params("arbitrary", "arbitrary"),
        name="ffn",
    )(x, mod, g1, w1, w2, g2)


def _placement_constants(tm, page, t_s):
    ii = np.arange(tm)
    triu = (ii[:, None] <= ii[None, :]).astype(np.float32)
    jj = np.arange(page)
    mstrict = (jj[:, None] > jj[None, :]).astype(np.float32)
    pq = np.zeros((N_HEADS * EXTRA, LANES), np.float32)
    pk = np.zeros((N_HEADS * EXTRA, LANES), np.float32)
    for h in range(N_HEADS):
        for p in range(N_PIECES):
            pq[h * EXTRA + p, ONES_SLOT] = 1.0
            pq[h * EXTRA + N_PIECES + p, p * N_HEADS + h] = 1.0
            pk[h * EXTRA + p, p * N_HEADS + h] = -1.0
            pk[h * EXTRA + N_PIECES + p, ONES_SLOT] = 1.0
    head_of_col = np.arange(D_ATTN) // HEAD_DIM
    bd_mask = head_of_col[None, :] == (np.arange(t_s * N_HEADS) % N_HEADS)[:, None]
    bf = lambda a: jnp.asarray(a, BF16)
    return dict(triu=bf(triu), mstrict=bf(mstrict), pq=bf(pq), pk=bf(pk), bd_mask=jnp.asarray(bd_mask))


def kernel(x_prompt, x_sample, cache_k, cache_v, cache_logf, state_pool, page_table, c_prompt, c_sample, w_ada, b_ada, g_pre_mix, w_in, b_forget, w_pool_mix, pool_scale, w_attn_up, w_pool_up, w_out, g_post_mix, g_pre_ffn, w_ff1, w_ff2, g_post_ffn):
    depth = w_in.shape[0]
    nb_p, t_p, d = x_prompt.shape
    nb_s, t_s, _ = x_sample.shape
    m_s = nb_s * t_s
    n_phys, page = cache_k.shape[1], cache_k.shape[2]
    n_pages = page_table.shape[1]
    assert t_p % HALO == 0 and t_p >= HALO and t_s <= POOL_BUF

    tm = min(512, t_p)
    bq, bk = min(512, t_p), min(1024, t_p)
    n_pg = min(16, n_pages)
    assert t_p % tm == 0 and t_p % bk == 0 and bk % bq == 0 and n_pages % n_pg == 0

    n_c = nb_p + nb_s
    pad = (-n_c) % 8
    c_all = jnp.concatenate([c_prompt, c_sample, jnp.zeros((pad, d), F32)], axis=0)
    mod_all = _modulation(c_all, w_ada, b_ada)

    ckT = cache_k.transpose(0, 1, 3, 4, 2).reshape(depth, n_phys, D_ATTN, page)
    cvT = cache_v.transpose(0, 1, 3, 4, 2).reshape(depth, n_phys, D_ATTN, page)
    clfT = cache_logf.transpose(0, 1, 3, 2)
    pt_flat = page_table.reshape(-1).astype(jnp.int32)

    cst = _placement_constants(tm, page, t_s)

    o_q, o_k, o_v, o_f = D_ATTN, 2 * D_ATTN, 3 * D_ATTN, 3 * D_ATTN + N_HEADS
    o_u = o_f + D_POOL

    xp = x_prompt
    xs = x_sample.reshape(1, m_s, d)
    prev = tuple(jnp.zeros((depth, nb_p, n, t_p), F32) for n in (D_ATTN, D_ATTN, N_HEADS))
    pp, ks, vs, ls, ps = [], [], [], [], []
    for l in range(depth):
        wl = w_in[l]
        wq = (wl[:, :o_q] * QK_SCALE).astype(BF16)
        wkv = wl[:, o_q:o_v].astype(BF16)
        wf = wl[:, o_v:o_f]
        bfl = b_forget[l].reshape(1, N_HEADS)
        wf_pad = jnp.pad(wf, ((0, 0), (0, LANES - N_HEADS))).astype(BF16)
        bf_pad = jnp.pad(bfl, ((0, 0), (0, LANES - N_HEADS)))
        wfT_rep = jnp.pad(jnp.tile(wf.T, (N_PIECES, 1)), ((0, N_HEADS), (0, 0)))
        bfT_rep = jnp.pad(jnp.tile(bfl.T, (N_PIECES, 1)), ((0, N_HEADS), (0, 0)))
        wT = jnp.concatenate([(wl[:, :o_q] * (QK_SCALE * LOG2E)).T, wl[:, o_q:o_v].T, wfT_rep], axis=0).astype(BF16)
        wu = wl[:, o_f:o_u].astype(BF16)
        wgab = wl[:, o_u:].astype(BF16)
        wmix = w_pool_mix[l].astype(BF16)
        pscale = pool_scale[l].reshape(1, D_POOL)
        wau = w_attn_up[l].astype(BF16)
        wpu = w_pool_up[l].astype(BF16)
        wout = w_out[l].astype(BF16)
        w1 = w_ff1[l].astype(BF16)
        w2 = w_ff2[l].astype(BF16)
        g_pm = g_pre_mix[l].reshape(1, d)
        g_qm = g_post_mix[l].reshape(1, d)
        g_pf = g_pre_ffn[l].reshape(1, d)
        g_qf = g_post_ffn[l].reshape(1, d)

        mod_p = mod_all[l, :nb_p].reshape(nb_p, 6, d)
        mod_s = jnp.repeat(mod_all[l, nb_p:n_c].reshape(nb_s, 6, d), t_s, axis=0).transpose(1, 0, 2)

        consts = (g_pm, wT, bfT_rep, wu, wgab, cst["triu"], cst["pq"], cst["pk"])
        qaT, ka, vaT, kT_all, vT_all, lfT_all, dpool, ga, gb, ulast = _in_prompt(
            xp, mod_p, consts, prev, layer=l, depth=depth, tm=tm)
        prev = (kT_all, vT_all, lfT_all)
        oT = _attention_prompt(qaT, ka, vaT, bq=bq, bk=bk)
        xp = _post(oT, dpool, ga, gb, xp, mod_p, wmix, pscale, wau, wpu, wout, g_qm, tm=tm, per_row=False,
                   o_transposed=True)
        xp = _ffn(xp, mod_p, g_pf, w1, w2, g_qf, tm=tm, per_row=False)
        pp.append(ulast[:, HALO - POOL_BUF:, :])

        q_s, k_s, v_s, lf_s, u_s, ga_s, gb_s = _in_sample(xs[0], mod_s, g_pm, wq, wkv, wf_pad, bf_pad, wu, wgab)
        lf_s = lf_s[:, :N_HEADS].reshape(nb_s, t_s, N_HEADS)
        ext = jnp.concatenate([state_pool[l], u_s.reshape(nb_s, t_s, D_POOL)], axis=1)
        d_s = _pool_sample(ext, t_s)
        qbd = jnp.where(cst["bd_mask"][None], jnp.repeat(q_s.reshape(nb_s, t_s, 1, D_ATTN), N_HEADS, axis=2)
                        .reshape(nb_s, t_s * N_HEADS, D_ATTN), 0.0).astype(BF16)
        o_s = _attention_sample(pt_flat, qbd, k_s.reshape(nb_s, t_s, D_ATTN), v_s.reshape(nb_s, t_s, D_ATTN),
                                lf_s.transpose(0, 2, 1), cst["mstrict"], ckT, cvT, clfT,
                                layer=l, n_pages=n_pages, n_pg=n_pg)
        xs = _post(o_s.reshape(1, m_s, D_ATTN), d_s.reshape(1, m_s, D_POOL), ga_s[None], gb_s[None], xs, mod_s,
                   wmix, pscale, wau, wpu, wout, g_qm, tm=m_s, per_row=True, o_transposed=False)
        xs = _ffn(xs, mod_s, g_pf, w1, w2, g_qf, tm=m_s, per_row=True)
        ks.append(k_s.reshape(nb_s, t_s, N_HEADS, HEAD_DIM))
        vs.append(v_s.reshape(nb_s, t_s, N_HEADS, HEAD_DIM))
        ls.append(lf_s)
        ps.append(ext[:, t_s:, :])

    kT_all, vT_all, lfT_all = prev

    def heads_last(a):
        return a.reshape(depth, nb_p, N_HEADS, HEAD_DIM, t_p).transpose(0, 1, 4, 2, 3)

    return (xp, xs.reshape(nb_s, t_s, d),
            heads_last(kT_all), heads_last(vT_all), lfT_all.transpose(0, 1, 3, 2), jnp.stack(pp),
            jnp.stack(ks), jnp.stack(vs), jnp.stack(ls), jnp.stack(ps))
```
